```python
import math
import jax, jax.numpy as jnp
from jax import lax
import numpy as np

D_MODEL = 2048
BATCH = 4
SEQ = 4096
DEPTH = 2

N_MIXERS = 2
N_HEADS = 8
HEAD_DIM_QK = 128
HEAD_DIM_V = 256
Q_BLOCK = 128
D_RNN = D_MODEL
N_RNN_BLOCKS = 8
RNN_BLOCK = D_RNN // N_RNN_BLOCKS
CONV_WIDTH = 4
RG_C = 8.0
D_FF = 4 * D_MODEL
D_PLE = 256
N_ATTN = (DEPTH + 1) // 2
N_REC = DEPTH // 2
EPS = 1e-6

kernel_name = 'hybrid_diffattn_rglru_sqrelu_ple'


def rmsnorm(x, g):
    xf = x.astype(jnp.float32)
    y = xf * lax.rsqrt(jnp.mean(xf * xf, axis=-1, keepdims=True) + EPS)
    return (y * g.astype(jnp.float32)).astype(x.dtype)


def lambda_init(layer_idx):
    return 0.8 - 0.6 * math.exp(-0.3 * layer_idx)


def diff_attention(xn, w_qkv, g_q, g_k, lam_q1, lam_k1, lam_q2, lam_k2, g_sub, w_o, lam0):
    B, S, _ = xn.shape
    nq = N_HEADS * 2 * HEAD_DIM_QK
    qkv = xn @ w_qkv
    q, k, v = jnp.split(qkv, [nq, 2 * nq], axis=-1)
    q = rmsnorm(q.reshape(B, S, N_HEADS, 2, HEAD_DIM_QK), g_q)
    k = rmsnorm(k.reshape(B, S, N_HEADS, 2, HEAD_DIM_QK), g_k)
    v = v.reshape(B, S, N_HEADS, HEAD_DIM_V)
    lam = (jnp.exp(jnp.sum(lam_q1.astype(jnp.float32) * lam_k1.astype(jnp.float32)))
           - jnp.exp(jnp.sum(lam_q2.astype(jnp.float32) * lam_k2.astype(jnp.float32)))
           + lam0)
    slopes = jnp.exp2(-8.0 * jnp.arange(1, N_HEADS + 1, dtype=jnp.float32) / N_HEADS)
    scale = HEAD_DIM_QK ** -0.5
    n_blk = S // Q_BLOCK
    qb = q.reshape(B, n_blk, Q_BLOCK, N_HEADS, 2, HEAD_DIM_QK).transpose(1, 0, 2, 3, 4, 5)
    k_pos = jnp.arange(S)

    def block(args):
        q_blk, start = args
        s = jnp.einsum('bqhcd,bkhcd->bhcqk', q_blk, k,
                       preferred_element_type=jnp.float32) * scale
        q_pos = start + jnp.arange(Q_BLOCK)
        dist = (q_pos[:, None] - k_pos[None, :]).astype(jnp.float32)
        s = s - slopes[None, :, None, None, None] * dist
        s = jnp.where(dist >= 0, s, -jnp.inf)
        pr = jax.nn.softmax(s, axis=-1)
        attn = pr[:, :, 0] - lam * pr[:, :, 1]
        return jnp.einsum('bhqk,bkhd->bqhd', attn, v,
                          preferred_element_type=jnp.float32)

    o = lax.map(block, (qb, jnp.arange(n_blk) * Q_BLOCK))
    o = o.transpose(1, 0, 2, 3, 4).reshape(B, S, N_HEADS, HEAD_DIM_V)
    o = rmsnorm(o, g_sub) * (1.0 - lam0)
    return o.reshape(B, S, N_HEADS * HEAD_DIM_V).astype(xn.dtype) @ w_o


def rglru_block(xn, w_in, conv_w, conv_b, w_ga, b_ga, w_gx, b_gx, lam, w_o):
    B, S, _ = xn.shape
    u = xn @ w_in
    gate, xr = jnp.split(u, 2, axis=-1)
    y = jax.nn.gelu(gate, approximate=True)
    xc = lax.conv_general_dilated(
        xr, conv_w[:, None, :].astype(xr.dtype), window_strides=(1,),
        padding=[(CONV_WIDTH - 1, 0)], dimension_numbers=('NWC', 'WIO', 'NWC'),
        feature_group_count=D_RNN) + conv_b
    xb = xc.reshape(B, S, N_RNN_BLOCKS, RNN_BLOCK)
    r = jax.nn.sigmoid((jnp.einsum('bsni,nij->bsnj', xb, w_ga).reshape(B, S, D_RNN)
                        + b_ga).astype(jnp.float32))
    ig = jax.nn.sigmoid((jnp.einsum('bsni,nij->bsnj', xb, w_gx).reshape(B, S, D_RNN)
                         + b_gx).astype(jnp.float32))
    log_a = -RG_C * r * jax.nn.softplus(-lam.astype(jnp.float32))
    a = jnp.exp(log_a)
    b = jnp.sqrt(-jnp.expm1(2.0 * log_a)) * (ig * xc.astype(jnp.float32))

    def combine(left, right):
        a1, b1 = left
        a2, b2 = right
        return a1 * a2, a2 * b1 + b2

    _, h = lax.associative_scan(combine, (a, b), axis=1)
    return (h.astype(xn.dtype) * y) @ w_o


def sqrelu_mlp(xn, w_up, w_down):
    return jnp.square(jax.nn.relu(xn @ w_up)) @ w_down


def setup_inputs(seed: int = 0) -> dict:
    key = jax.random.key(seed)
    ks = jax.random.split(key, 32)
    f32 = jnp.float32
    D, H, DK, DV = D_MODEL, N_HEADS, HEAD_DIM_QK, HEAD_DIM_V

    def nrm(k, shape, scale):
        return jax.random.normal(k, shape, f32) * scale

    def gain(k, shape):
        return 1.0 + 0.05 * jax.random.normal(k, shape, f32)

    a_c = jax.random.uniform(ks[20], (N_REC, D_RNN), f32, 0.9, 0.999)
    s = a_c ** (1.0 / RG_C)
    lam_rec = jnp.log(s) - jnp.log1p(-s)
    return {
        'x': jax.random.normal(ks[0], (BATCH, SEQ, D), f32),
        'p': jax.random.normal(ks[1], (DEPTH, BATCH, SEQ, D_PLE), f32),
        'g_mix': gain(ks[2], (DEPTH, D)),
        'g_mlp': gain(ks[3], (DEPTH, D)),
        'g_ple': gain(ks[4], (DEPTH, D)),
        'w_qkv': nrm(ks[5], (N_ATTN, D, 4 * H * DK + H * DV), D ** -0.5),
        'g_q': gain(ks[6], (N_ATTN, DK)),
        'g_k': gain(ks[7], (N_ATTN, DK)),
        'lam_q1': nrm(ks[8], (N_ATTN, DK), 0.1),
        'lam_k1': nrm(ks[9], (N_ATTN, DK), 0.1),
        'lam_q2': nrm(ks[10], (N_ATTN, DK), 0.1),
        'lam_k2': nrm(ks[11], (N_ATTN, DK), 0.1),
        'g_sub': gain(ks[12], (N_ATTN, DV)),
        'w_o_attn': nrm(ks[13], (N_ATTN, H * DV, D), (H * DV) ** -0.5),
        'w_in_rec': nrm(ks[14], (N_REC, D, 2 * D_RNN), D ** -0.5),
        'conv_w': nrm(ks[15], (N_REC, CONV_WIDTH, D_RNN), CONV_WIDTH ** -0.5),
        'conv_b': nrm(ks[16], (N_REC, D_RNN), 0.01),
        'w_gate_a': nrm(ks[17], (N_REC, N_RNN_BLOCKS, RNN_BLOCK, RNN_BLOCK), RNN_BLOCK ** -0.5),
        'b_gate_a': nrm(ks[18], (N_REC, D_RNN), 0.01),
        'w_gate_x': nrm(ks[19], (N_REC, N_RNN_BLOCKS, RNN_BLOCK, RNN_BLOCK), RNN_BLOCK ** -0.5),
        'b_gate_x': nrm(ks[21], (N_REC, D_RNN), 0.01),
        'lam_rec': lam_rec,
        'w_o_rec': nrm(ks[22], (N_REC, D_RNN, D), D_RNN ** -0.5),
        'w_up': nrm(ks[23], (DEPTH, D, D_FF), D ** -0.5),
        'w_down': nrm(ks[24], (DEPTH, D_FF, D), D_FF ** -0.5),
        'w_ple_proj': nrm(ks[25], (DEPTH, D_PLE, D), D_PLE ** -0.5),
        'w_ple_gate': nrm(ks[26], (DEPTH, D, D), D ** -0.5),
    }


def reference(x, p, g_mix, g_mlp, g_ple, w_qkv, g_q, g_k, lam_q1, lam_k1, lam_q2,
              lam_k2, g_sub, w_o_attn, w_in_rec, conv_w, conv_b, w_gate_a, b_gate_a,
              w_gate_x, b_gate_x, lam_rec, w_o_rec, w_up, w_down, w_ple_proj,
              w_ple_gate):
    h = x
    for i in range(DEPTH):
        xn = rmsnorm(h, g_mix[i])
        j = i // N_MIXERS
        if i % N_MIXERS == 0:
            mix = diff_attention(xn, w_qkv[j], g_q[j], g_k[j], lam_q1[j], lam_k1[j],
                                 lam_q2[j], lam_k2[j], g_sub[j], w_o_attn[j],
                                 lambda_init(i))
        else:
            mix = rglru_block(xn, w_in_rec[j], conv_w[j], conv_b[j], w_gate_a[j],
                              b_gate_a[j], w_gate_x[j], b_gate_x[j], lam_rec[j],
                              w_o_rec[j])
        h = h + mix
        h = h + sqrelu_mlp(rmsnorm(h, g_mlp[i]), w_up[i], w_down[i])
        ple_gate = jax.nn.sigmoid(rmsnorm(h, g_ple[i]) @ w_ple_gate[i])
        h = h + ple_gate * (p[i] @ w_ple_proj[i])
    return h
```

```python
import functools
import math

import jax
import jax.numpy as jnp
from jax import lax
from jax.experimental import pallas as pl
from jax.experimental.pallas import tpu as pltpu

D_MODEL = 2048
N_HEADS = 8
HEAD_DIM_QK = 128
HEAD_DIM_V = 256
D_RNN = D_MODEL
N_RNN_BLOCKS = 8
RNN_BLOCK = D_RNN // N_RNN_BLOCKS
CONV_WIDTH = 4
RG_C = 8.0
D_FF = 4 * D_MODEL
D_PLE = 256
EPS = 1e-6

F32 = jnp.float32
BF16 = jnp.bfloat16

SUBLANES = 8
LANES = 128
VMEM_LIMIT_BYTES = 56 * 1024 * 1024

TM_PROJ = 1024
TN_PROJ = 1024
TM_MLP = 512
TF_MLP = 1024
TM_PLE = 512
TN_PLE = 1024
T_ATTN = 512
TS_REC = 512
SCAN_COLS = 512


def _compiler_params(n_axes):
    return pltpu.CompilerParams(
        dimension_semantics=("arbitrary",) * n_axes,
        vmem_limit_bytes=VMEM_LIMIT_BYTES)


def _rmsnorm_rows(x, g):
    ms = jnp.mean(x * x, axis=-1, keepdims=True)
    return x * lax.rsqrt(ms + EPS) * g


def _norm_matmul_kernel(h_ref, g_ref, w_ref, o_ref, xn_ref):
    @pl.when(pl.program_id(1) == 0)
    def _():
        xn_ref[...] = _rmsnorm_rows(h_ref[...], g_ref[...]).astype(BF16)

    acc = jnp.dot(xn_ref[...], w_ref[...], preferred_element_type=F32)
    o_ref[...] = acc.astype(o_ref.dtype)


def _qkv_kernel(h_ref, g_ref, w_ref, gqk_ref, o_ref, xn_ref, *, n_q_tiles, n_qk_tiles):
    j = pl.program_id(1)

    @pl.when(j == 0)
    def _():
        xn_ref[...] = _rmsnorm_rows(h_ref[...], g_ref[...]).astype(BF16)

    acc = jnp.dot(xn_ref[...], w_ref[...], preferred_element_type=F32)

    @pl.when(j < n_qk_tiles)
    def _():
        sc = jnp.where(j < n_q_tiles, HEAD_DIM_QK ** -0.5, 1.0).astype(F32)
        gain = gqk_ref[...] * sc
        for c in range(acc.shape[1] // HEAD_DIM_QK):
            blk = acc[:, c * HEAD_DIM_QK:(c + 1) * HEAD_DIM_QK]
            ms = jnp.mean(blk * blk, axis=-1, keepdims=True)
            o_ref[:, c * HEAD_DIM_QK:(c + 1) * HEAD_DIM_QK] = (
                blk * lax.rsqrt(ms + EPS) * gain).astype(o_ref.dtype)

    @pl.when(j >= n_qk_tiles)
    def _():
        o_ref[...] = acc.astype(o_ref.dtype)


def _norm_matmul(h, g, w, out_dtype):
    t, d = h.shape
    n = w.shape[1]
    tm, tn = TM_PROJ, TN_PROJ
    return pl.pallas_call(
        _norm_matmul_kernel,
        grid=(t // tm, n // tn),
        in_specs=[
            pl.BlockSpec((tm, d), lambda i, j: (i, 0)),
            pl.BlockSpec((1, d), lambda i, j: (0, 0)),
            pl.BlockSpec((d, tn), lambda i, j: (0, j)),
        ],
        out_specs=pl.BlockSpec((tm, tn), lambda i, j: (i, j)),
        out_shape=jax.ShapeDtypeStruct((t, n), out_dtype),
        scratch_shapes=[pltpu.VMEM((tm, d), BF16)],
        compiler_params=_compiler_params(2),
        name="norm_matmul",
    )(h, g, w)


def _qkv_proj(h, g, w, g_q, g_k):
    t, d = h.shape
    n = w.shape[1]
    tm, tn = TM_PROJ, TN_PROJ
    nq = N_HEADS * 2 * HEAD_DIM_QK
    n_q_tiles = nq // tn
    n_qk_tiles = 2 * nq // tn
    gqk = jnp.concatenate(
        [jnp.broadcast_to(g_q, (n_q_tiles, HEAD_DIM_QK)),
         jnp.broadcast_to(g_k, (n // tn - n_q_tiles, HEAD_DIM_QK))], axis=0)
    gqk = gqk.reshape(n // tn, 1, HEAD_DIM_QK)
    kern = functools.partial(_qkv_kernel, n_q_tiles=n_q_tiles, n_qk_tiles=n_qk_tiles)
    return pl.pallas_call(
        kern,
        grid=(t // tm, n // tn),
        in_specs=[
            pl.BlockSpec((tm, d), lambda i, j: (i, 0)),
            pl.BlockSpec((1, d), lambda i, j: (0, 0)),
            pl.BlockSpec((d, tn), lambda i, j: (0, j)),
            pl.BlockSpec((None, 1, HEAD_DIM_QK), lambda i, j: (j, 0, 0)),
        ],
        out_specs=pl.BlockSpec((tm, tn), lambda i, j: (i, j)),
        out_shape=jax.ShapeDtypeStruct((t, n), BF16),
        scratch_shapes=[pltpu.VMEM((tm, d), BF16)],
        compiler_params=_compiler_params(2),
        name="qkv_proj",
    )(h, g, w, gqk)


def _matmul_residual_kernel(a_ref, w_ref, h_ref, o_ref):
    o_ref[...] = h_ref[...] + jnp.dot(a_ref[...], w_ref[...], preferred_element_type=F32)


def _matmul_residual(a, w, h):
    t, k = a.shape
    n = w.shape[1]
    tm, tn = TM_PROJ, TN_PROJ
    return pl.pallas_call(
        _matmul_residual_kernel,
        grid=(t // tm, n // tn),
        in_specs=[
            pl.BlockSpec((tm, k), lambda i, j: (i, 0)),
            pl.BlockSpec((k, tn), lambda i, j: (0, j)),
            pl.BlockSpec((tm, tn), lambda i, j: (i, j)),
        ],
        out_specs=pl.BlockSpec((tm, tn), lambda i, j: (i, j)),
        out_shape=jax.ShapeDtypeStruct((t, n), F32),
        compiler_params=_compiler_params(2),
        name="matmul_residual",
    )(a, w, h)


def _mlp_kernel(h_ref, g_ref, wu_ref, wd_ref, o_ref, xn_ref):
    @pl.when(pl.program_id(1) == 0)
    def _():
        x = h_ref[...]
        xn_ref[...] = _rmsnorm_rows(x, g_ref[...]).astype(BF16)
        o_ref[...] = x

    u = jnp.dot(xn_ref[...], wu_ref[...], preferred_element_type=F32)
    a = jnp.square(jnp.maximum(u, 0.0)).astype(BF16)
    o_ref[...] += jnp.dot(a, wd_ref[...], preferred_element_type=F32)


def _mlp(h, g, w_up, w_down):
    t, d = h.shape
    f = w_up.shape[1]
    tm, tf = TM_MLP, TF_MLP
    return pl.pallas_call(
        _mlp_kernel,
        grid=(t // tm, f // tf),
        in_specs=[
            pl.BlockSpec((tm, d), lambda i, j: (i, 0)),
            pl.BlockSpec((1, d), lambda i, j: (0, 0)),
            pl.BlockSpec((d, tf), lambda i, j: (0, j)),
            pl.BlockSpec((tf, d), lambda i, j: (j, 0)),
        ],
        out_specs=pl.BlockSpec((tm, d), lambda i, j: (i, 0)),
        out_shape=jax.ShapeDtypeStruct((t, d), F32),
        scratch_shapes=[pltpu.VMEM((tm, d), BF16)],
        compiler_params=_compiler_params(2),
        name="mlp",
    )(h, g, w_up, w_down)


def _ple_kernel(h_ref, hblk_ref, g_ref, wg_ref, p_ref, wp_ref, o_ref, xn_ref):
    @pl.when(pl.program_id(1) == 0)
    def _():
        xn_ref[...] = _rmsnorm_rows(h_ref[...], g_ref[...]).astype(BF16)

    gate = jax.nn.sigmoid(jnp.dot(xn_ref[...], wg_ref[...], preferred_element_type=F32))
    proj = jnp.dot(p_ref[...].astype(BF16), wp_ref[...], preferred_element_type=F32)
    o_ref[...] = hblk_ref[...] + gate * proj


def _ple(h, g, w_gate, p, w_proj):
    t, d = h.shape
    dp = p.shape[1]
    tm, tn = TM_PLE, TN_PLE
    return pl.pallas_call(
        _ple_kernel,
        grid=(t // tm, d // tn),
        in_specs=[
            pl.BlockSpec((tm, d), lambda i, j: (i, 0)),
            pl.BlockSpec((tm, tn), lambda i, j: (i, j)),
            pl.BlockSpec((1, d), lambda i, j: (0, 0)),
            pl.BlockSpec((d, tn), lambda i, j: (0, j)),
            pl.BlockSpec((tm, dp), lambda i, j: (i, 0)),
            pl.BlockSpec((dp, tn), lambda i, j: (0, j)),
        ],
        out_specs=pl.BlockSpec((tm, tn), lambda i, j: (i, j)),
        out_shape=jax.ShapeDtypeStruct((t, d), F32),
        scratch_shapes=[pltpu.VMEM((tm, d), BF16)],
        compiler_params=_compiler_params(2),
        name="ple",
    )(h, h, g, w_gate, p, w_proj)


def _attn_kernel(slopes_ref, q_ref, k_ref, v_ref, lq1_ref, lk1_ref, lq2_ref, lk2_ref,
                 gsub_ref, o_ref, m1_ref, l1_ref, acc1_ref, m2_ref, l2_ref, acc2_ref,
                 *, lam0):
    tq = q_ref.shape[1]
    tk = tq
    head = pl.program_id(1)
    i = pl.program_id(2)
    slope = slopes_ref[head]

    state = ((m1_ref, l1_ref, acc1_ref), (m2_ref, l2_ref, acc2_ref))
    for m_ref, l_ref, acc_ref in state:
        m_ref[...] = jnp.full(m_ref.shape, -jnp.inf, F32)
        l_ref[...] = jnp.zeros(l_ref.shape, F32)
        acc_ref[...] = jnp.zeros(acc_ref.shape, F32)

    q = q_ref[0]
    col = lax.broadcasted_iota(jnp.int32, (1, tk), 1)

    def step(j, masked):
        start = pl.multiple_of(j * tk, tk)
        k = k_ref[0, pl.ds(start, tk), :]
        v = v_ref[0, pl.ds(start, tk), :]
        rel = (j * tk - i * tq + col).astype(F32)
        bias = slope * rel
        if masked:
            row_i = lax.broadcasted_iota(jnp.int32, (tq, tk), 0)
            col_i = lax.broadcasted_iota(jnp.int32, (tq, tk), 1)
            visible = col_i <= row_i
        for c, (m_ref, l_ref, acc_ref) in enumerate(state):
            qc = q[:, c * HEAD_DIM_QK:(c + 1) * HEAD_DIM_QK]
            kc = k[:, c * HEAD_DIM_QK:(c + 1) * HEAD_DIM_QK]
            s = lax.dot_general(qc, kc, (((1,), (1,)), ((), ())),
                                preferred_element_type=F32)
            s = s + bias
            if masked:
                s = jnp.where(visible, s, -jnp.inf)
            m_prev = m_ref[...]
            m_new = jnp.maximum(m_prev, jnp.max(s, axis=1, keepdims=True))
            alpha = jnp.exp(m_prev - m_new)
            p = jnp.exp(s - m_new)
            l_ref[...] = alpha * l_ref[...] + jnp.sum(p, axis=1, keepdims=True)
            acc_ref[...] = alpha * acc_ref[...] + jnp.dot(
                p.astype(BF16), v, preferred_element_type=F32)
            m_ref[...] = m_new

    def body(j, carry):
        step(j, masked=False)
        return carry

    lax.fori_loop(0, i, body, 0)
    step(i, masked=True)

    lam = (jnp.exp(jnp.sum(lq1_ref[...] * lk1_ref[...], axis=-1, keepdims=True))
           - jnp.exp(jnp.sum(lq2_ref[...] * lk2_ref[...], axis=-1, keepdims=True))
           + lam0)
    o = acc1_ref[...] * (1.0 / l1_ref[...]) - lam * (acc2_ref[...] * (1.0 / l2_ref[...]))
    o = _rmsnorm_rows(o, gsub_ref[...]) * (1.0 - lam0)
    o_ref[0] = o.astype(o_ref.dtype)


def _diff_attention(qkv, slopes, lam_q1, lam_k1, lam_q2, lam_k2, g_sub, lam0, batch, seq):
    tq = T_ATTN
    nh = N_HEADS
    blk = 2 * HEAD_DIM_QK
    assert blk == HEAD_DIM_V
    vec = lambda: pl.BlockSpec((1, HEAD_DIM_QK), lambda b, h, i: (0, 0))
    kern = functools.partial(_attn_kernel, lam0=lam0)
    return pl.pallas_call(
        kern,
        grid=(batch, nh, seq // tq),
        in_specs=[
            pl.BlockSpec(memory_space=pltpu.SMEM),
            pl.BlockSpec((1, tq, blk), lambda b, h, i: (b, i, h)),
            pl.BlockSpec((1, seq, blk), lambda b, h, i: (b, 0, nh + h)),
            pl.BlockSpec((1, seq, blk), lambda b, h, i: (b, 0, 2 * nh + h)),
            vec(), vec(), vec(), vec(),
            pl.BlockSpec((1, HEAD_DIM_V), lambda b, h, i: (0, 0)),
        ],
        out_specs=pl.BlockSpec((1, tq, HEAD_DIM_V), lambda b, h, i: (b, i, h)),
        out_shape=jax.ShapeDtypeStruct((batch, seq, nh * HEAD_DIM_V), BF16),
        scratch_shapes=[
            pltpu.VMEM((tq, 1), F32), pltpu.VMEM((tq, 1), F32),
            pltpu.VMEM((tq, HEAD_DIM_V), F32),
            pltpu.VMEM((tq, 1), F32), pltpu.VMEM((tq, 1), F32),
            pltpu.VMEM((tq, HEAD_DIM_V), F32),
        ],
        compiler_params=_compiler_params(3),
        name="diff_attention",
    )(slopes, qkv, qkv, qkv, lam_q1, lam_k1, lam_q2, lam_k2, g_sub)


def _rglru_kernel(gate_ref, xr_ref, cw_ref, cb_ref, wga_ref, bga_ref, wgx_ref, bgx_ref,
                  lam_ref, o_ref, xbuf_ref, a_ref, b_ref, hcar_ref):
    ts = xr_ref.shape[1]
    halo = SUBLANES

    @pl.when(pl.program_id(1) == 0)
    def _():
        xbuf_ref[0:halo, :] = jnp.zeros((halo, D_RNN), F32)
        hcar_ref[...] = jnp.zeros(hcar_ref.shape, F32)

    xbuf_ref[halo:halo + ts, :] = xr_ref[0]
    cw = cw_ref[...]
    xc = cb_ref[...] + cw[CONV_WIDTH - 1:CONV_WIDTH, :] * xbuf_ref[halo:halo + ts, :]
    for back in range(1, CONV_WIDTH):
        w_row = cw[CONV_WIDTH - 1 - back:CONV_WIDTH - back, :]
        xc = xc + w_row * xbuf_ref[halo - back:halo - back + ts, :]
    xbuf_ref[0:halo, :] = xbuf_ref[ts:ts + halo, :]

    neg_c_softplus = -RG_C * jax.nn.softplus(-lam_ref[...])
    for n in range(N_RNN_BLOCKS):
        sl = slice(n * RNN_BLOCK, (n + 1) * RNN_BLOCK)
        xcn = xc[:, sl]
        xb = xcn.astype(BF16)
        r = jax.nn.sigmoid(
            jnp.dot(xb, wga_ref[n], preferred_element_type=F32) + bga_ref[:, sl])
        ig = jax.nn.sigmoid(
            jnp.dot(xb, wgx_ref[n], preferred_element_type=F32) + bgx_ref[:, sl])
        log_a = neg_c_softplus[:, sl] * r
        a = jnp.exp(log_a)
        one_minus_a2 = -jnp.tanh(log_a) * (a * a + 1.0)
        a_ref[:, sl] = a
        b_ref[:, sl] = jnp.sqrt(one_minus_a2) * (ig * xcn)

    sub = lax.broadcasted_iota(jnp.int32, (SUBLANES, SCAN_COLS), 0)
    for cblk in range(D_RNN // SCAN_COLS):
        cs = slice(cblk * SCAN_COLS, (cblk + 1) * SCAN_COLS)

        def tile_step(t, h_prev, cs=cs):
            r0 = pl.multiple_of(t * SUBLANES, SUBLANES)
            a = a_ref[pl.ds(r0, SUBLANES), cs]
            b = b_ref[pl.ds(r0, SUBLANES), cs]
            for d in (1, 2, 4):
                keep = sub >= d
                a_sh = jnp.where(keep, pltpu.roll(a, d, axis=0), 1.0)
                b_sh = jnp.where(keep, pltpu.roll(b, d, axis=0), 0.0)
                b = a * b_sh + b
                a = a * a_sh
            h = a * h_prev + b
            b_ref[pl.ds(r0, SUBLANES), cs] = h
            return jnp.broadcast_to(h[SUBLANES - 1:SUBLANES, :], (SUBLANES, SCAN_COLS))

        hcar_ref[:, cs] = lax.fori_loop(0, ts // SUBLANES, tile_step, hcar_ref[:, cs],
                                        unroll=4)

    y = jax.nn.gelu(gate_ref[0], approximate=True)
    o_ref[0] = (b_ref[...] * y).astype(o_ref.dtype)


def _rglru(u, conv_w, conv_b, w_ga, b_ga, w_gx, b_gx, lam, batch, seq):
    ts = TS_REC
    row = lambda: pl.BlockSpec((1, D_RNN), lambda b, i: (0, 0))
    wblk = lambda: pl.BlockSpec((N_RNN_BLOCKS, RNN_BLOCK, RNN_BLOCK), lambda b, i: (0, 0, 0))
    return pl.pallas_call(
        _rglru_kernel,
        grid=(batch, seq // ts),
        in_specs=[
            pl.BlockSpec((1, ts, D_RNN), lambda b, i: (b, i, 0)),
            pl.BlockSpec((1, ts, D_RNN), lambda b, i: (b, i, 1)),
            pl.BlockSpec((CONV_WIDTH, D_RNN), lambda b, i: (0, 0)),
            row(), wblk(), row(), wblk(), row(), row(),
        ],
        out_specs=pl.BlockSpec((1, ts, D_RNN), lambda b, i: (b, i, 0)),
        out_shape=jax.ShapeDtypeStruct((batch, seq, D_RNN), BF16),
        scratch_shapes=[
            pltpu.VMEM((ts + SUBLANES, D_RNN), F32),
            pltpu.VMEM((ts, D_RNN), F32),
            pltpu.VMEM((ts, D_RNN), F32),
            pltpu.VMEM((SUBLANES, D_RNN), F32),
        ],
        compiler_params=_compiler_params(2),
        name="rglru",
    )(u, u, conv_w, conv_b, w_ga, b_ga, w_gx, b_gx, lam)


def _lambda_init(layer_idx):
    return 0.8 - 0.6 * math.exp(-0.3 * layer_idx)


def kernel(x, p, g_mix, g_mlp, g_ple, w_qkv, g_q, g_k, lam_q1, lam_k1, lam_q2, lam_k2,
           g_sub, w_o_attn, w_in_rec, conv_w, conv_b, w_gate_a, b_gate_a, w_gate_x,
           b_gate_x, lam_rec, w_o_rec, w_up, w_down, w_ple_proj, w_ple_gate):
    batch, seq, d = x.shape
    depth = p.shape[0]
    t = batch * seq
    h = x.reshape(t, d)
    row = lambda v: v.reshape(1, -1)
    slopes = jnp.exp2(-8.0 * jnp.arange(1, N_HEADS + 1, dtype=F32) / N_HEADS)

    for i in range(depth):
        j = i // 2
        if i % 2 == 0:
            qkv = _qkv_proj(h, row(g_mix[i]), w_qkv[j].astype(BF16), row(g_q[j]), row(g_k[j]))
            o = _diff_attention(
                qkv.reshape(batch, seq, -1), slopes, row(lam_q1[j]), row(lam_k1[j]),
                row(lam_q2[j]), row(lam_k2[j]), row(g_sub[j]), _lambda_init(i), batch, seq)
            h = _matmul_residual(o.reshape(t, -1), w_o_attn[j].astype(BF16), h)
        else:
            u = _norm_matmul(h, row(g_mix[i]), w_in_rec[j].astype(BF16), F32)
            hy = _rglru(
                u.reshape(batch, seq, -1), conv_w[j], row(conv_b[j]),
                w_gate_a[j].astype(BF16), row(b_gate_a[j]), w_gate_x[j].astype(BF16),
                row(b_gate_x[j]), row(lam_rec[j]), batch, seq)
            h = _matmul_residual(hy.reshape(t, -1), w_o_rec[j].astype(BF16), h)
        h = _mlp(h, row(g_mlp[i]), w_up[i].astype(BF16), w_down[i].astype(BF16))
        h = _ple(h, row(g_ple[i]), w_ple_gate[i].astype(BF16), p[i].reshape(t, -1),
                 w_ple_proj[i].astype(BF16))
    return h.reshape(batch, seq, d)
```

```python
import functools
import math

import jax
import jax.numpy as jnp
from jax import lax
from jax.experimental import pallas as pl
from jax.experimental.pallas import tpu as pltpu

D_MODEL = 2048
N_HEADS = 8
HEAD_DIM_QK = 128
HEAD_DIM_V = 256
D_RNN = D_MODEL
N_RNN_BLOCKS = 8
RNN_BLOCK = D_RNN // N_RNN_BLOCKS
CONV_WIDTH = 4
RG_C = 8.0
D_FF = 4 * D_MODEL
D_PLE = 256
EPS = 1e-6
LOG2E = math.log2(math.e)

F32 = jnp.float32
BF16 = jnp.bfloat16

SUBLANES = 8
LANES = 128
VMEM_LIMIT_BYTES = 56 * 1024 * 1024

TM_PROJ = 1024
TN_PROJ = 1024
TM_RES = 512
TN_CHUNK = 512
TM_MLP = 512
TF_MLP = 1024
T_ATTN = 512
TS_REC = 512
SCAN_COLS = 512


def _compiler_params(n_axes):
    return pltpu.CompilerParams(
        dimension_semantics=("arbitrary",) * n_axes,
        vmem_limit_bytes=VMEM_LIMIT_BYTES)


def _resident(block_shape, index_map):
    return pl.BlockSpec(block_shape, index_map, pipeline_mode=pl.Buffered(1))


def _rmsnorm_rows(x, g):
    ms = jnp.mean(x * x, axis=-1, keepdims=True)
    return x * lax.rsqrt(ms + EPS) * g


def _norm_matmul_kernel(h_ref, g_ref, w_ref, o_ref, xn_ref):
    @pl.when(pl.program_id(1) == 0)
    def _():
        xn_ref[...] = _rmsnorm_rows(h_ref[...], g_ref[...]).astype(BF16)

    o_ref[...] = jnp.dot(xn_ref[...], w_ref[...],
                         preferred_element_type=F32).astype(o_ref.dtype)


def _norm_matmul(h, g, w, out_dtype):
    t, d = h.shape
    n = w.shape[1]
    tm, tn = TM_PROJ, TN_PROJ
    return pl.pallas_call(
        _norm_matmul_kernel,
        grid=(t // tm, n // tn),
        in_specs=[
            pl.BlockSpec((tm, d), lambda i, j: (i, 0)),
            pl.BlockSpec((1, d), lambda i, j: (0, 0)),
            pl.BlockSpec((d, tn), lambda i, j: (0, j)),
        ],
        out_specs=pl.BlockSpec((tm, tn), lambda i, j: (i, j)),
        out_shape=jax.ShapeDtypeStruct((t, n), out_dtype),
        scratch_shapes=[pltpu.VMEM((tm, d), BF16)],
        compiler_params=_compiler_params(2),
        name="norm_matmul",
    )(h, g, w)


def _qkv_kernel(h_ref, g_ref, wqk_ref, wvt_ref, gq_ref, gk_ref, q_ref, k_ref, vt_ref,
                xn_ref, *, n_q_tiles, n_qk_tiles):
    j = pl.program_id(1)

    @pl.when(j == 0)
    def _():
        xn_ref[...] = _rmsnorm_rows(h_ref[...], g_ref[...]).astype(BF16)

    def qk_norm_to(out_ref, gain):
        acc = jnp.dot(xn_ref[...], wqk_ref[...], preferred_element_type=F32)
        for c in range(acc.shape[1] // HEAD_DIM_QK):
            cs = slice(c * HEAD_DIM_QK, (c + 1) * HEAD_DIM_QK)
            blk = acc[:, cs]
            ms = jnp.mean(blk * blk, axis=-1, keepdims=True)
            out_ref[:, cs] = (blk * lax.rsqrt(ms + EPS) * gain).astype(out_ref.dtype)

    @pl.when(j < n_q_tiles)
    def _():
        qk_norm_to(q_ref, gq_ref[...] * (HEAD_DIM_QK ** -0.5 * LOG2E))

    @pl.when(jnp.logical_and(j >= n_q_tiles, j < n_qk_tiles))
    def _():
        qk_norm_to(k_ref, gk_ref[...])

    @pl.when(j >= n_qk_tiles)
    def _():
        vt = lax.dot_general(wvt_ref[...], xn_ref[...], (((1,), (1,)), ((), ())),
                             preferred_element_type=F32)
        tk = vt_ref.shape[2]
        for c in range(vt_ref.shape[0]):
            vt_ref[c] = vt[:, c * tk:(c + 1) * tk].astype(vt_ref.dtype)


def _qkv_proj(h, g, w_qk, w_vt, g_q, g_k):
    t, d = h.shape
    nqk = w_qk.shape[1]
    nv = w_vt.shape[0]
    tm, tn = TM_PROJ, TN_PROJ
    tk = T_ATTN
    nq = nqk // 2
    n_q_tiles = nq // tn
    n_qk_tiles = nqk // tn
    n_tiles = n_qk_tiles + nv // tn
    kern = functools.partial(_qkv_kernel, n_q_tiles=n_q_tiles, n_qk_tiles=n_qk_tiles)
    clamp = lambda v, lo, hi: jnp.minimum(jnp.maximum(v, lo), hi)
    return pl.pallas_call(
        kern,
        grid=(t // tm, n_tiles),
        in_specs=[
            pl.BlockSpec((tm, d), lambda i, j: (i, 0)),
            pl.BlockSpec((1, d), lambda i, j: (0, 0)),
            pl.BlockSpec((d, tn), lambda i, j: (0, jnp.minimum(j, n_qk_tiles - 1))),
            pl.BlockSpec((tn, d), lambda i, j: (clamp(j - n_qk_tiles, 0, nv // tn - 1), 0)),
            pl.BlockSpec((1, HEAD_DIM_QK), lambda i, j: (0, 0)),
            pl.BlockSpec((1, HEAD_DIM_QK), lambda i, j: (0, 0)),
        ],
        out_specs=[
            pl.BlockSpec((tm, tn), lambda i, j: (i, jnp.minimum(j, n_q_tiles - 1))),
            pl.BlockSpec((tm, tn), lambda i, j: (i, clamp(j - n_q_tiles, 0, n_q_tiles - 1))),
            pl.BlockSpec((tm // tk, tn, tk),
                         lambda i, j: (i, clamp(j - n_qk_tiles, 0, nv // tn - 1), 0)),
        ],
        out_shape=[
            jax.ShapeDtypeStruct((t, nq), BF16),
            jax.ShapeDtypeStruct((t, nq), BF16),
            jax.ShapeDtypeStruct((t // tk, nv, tk), BF16),
        ],
        scratch_shapes=[pltpu.VMEM((tm, d), BF16)],
        compiler_params=_compiler_params(2),
        name="qkv_proj",
    )(h, g, w_qk, w_vt, g_q, g_k)


def _matmul_residual_kernel(a_ref, w_ref, h_ref, o_ref):
    a = a_ref[...]
    for c in range(o_ref.shape[1] // TN_CHUNK):
        cs = slice(c * TN_CHUNK, (c + 1) * TN_CHUNK)
        o_ref[:, cs] = h_ref[:, cs] + jnp.dot(a, w_ref[:, cs], preferred_element_type=F32)


def _matmul_residual(a, w, h):
    t, k = a.shape
    n = w.shape[1]
    tm = TM_RES
    return pl.pallas_call(
        _matmul_residual_kernel,
        grid=(t // tm,),
        in_specs=[
            pl.BlockSpec((tm, k), lambda i: (i, 0)),
            _resident((k, n), lambda i: (0, 0)),
            pl.BlockSpec((tm, n), lambda i: (i, 0)),
        ],
        out_specs=pl.BlockSpec((tm, n), lambda i: (i, 0)),
        out_shape=jax.ShapeDtypeStruct((t, n), F32),
        compiler_params=_compiler_params(1),
        name="matmul_residual",
    )(a, w, h)


def _mlp_kernel(h_ref, g_ref, wu_ref, wd_ref, o_ref, xn_ref):
    @pl.when(pl.program_id(1) == 0)
    def _():
        x = h_ref[...]
        xn_ref[...] = _rmsnorm_rows(x, g_ref[...]).astype(BF16)
        o_ref[...] = x

    u = jnp.dot(xn_ref[...], wu_ref[...], preferred_element_type=F32)
    a = jnp.square(jnp.maximum(u, 0.0)).astype(BF16)
    o_ref[...] += jnp.dot(a, wd_ref[...], preferred_element_type=F32)


def _mlp(h, g, w_up, w_down):
    t, d = h.shape
    f = w_up.shape[1]
    tm, tf = TM_MLP, TF_MLP
    return pl.pallas_call(
        _mlp_kernel,
        grid=(t // tm, f // tf),
        in_specs=[
            pl.BlockSpec((tm, d), lambda i, j: (i, 0)),
            pl.BlockSpec((1, d), lambda i, j: (0, 0)),
            pl.BlockSpec((d, tf), lambda i, j: (0, j)),
            pl.BlockSpec((tf, d), lambda i, j: (j, 0)),
        ],
        out_specs=pl.BlockSpec((tm, d), lambda i, j: (i, 0)),
        out_shape=jax.ShapeDtypeStruct((t, d), F32),
        scratch_shapes=[pltpu.VMEM((tm, d), BF16)],
        compiler_params=_compiler_params(2),
        name="mlp",
    )(h, g, w_up, w_down)


def _ple_kernel(h_ref, g_ref, wg_ref, p_ref, wp_ref, o_ref):
    xn = _rmsnorm_rows(h_ref[...], g_ref[...]).astype(BF16)
    pb = p_ref[...].astype(BF16)
    for c in range(o_ref.shape[1] // TN_CHUNK):
        cs = slice(c * TN_CHUNK, (c + 1) * TN_CHUNK)
        gate = jax.nn.sigmoid(jnp.dot(xn, wg_ref[:, cs], preferred_element_type=F32))
        proj = jnp.dot(pb, wp_ref[:, cs], preferred_element_type=F32)
        o_ref[:, cs] = h_ref[:, cs] + gate * proj


def _ple(h, g, w_gate, p, w_proj):
    t, d = h.shape
    dp = p.shape[1]
    tm = TM_RES
    return pl.pallas_call(
        _ple_kernel,
        grid=(t // tm,),
        in_specs=[
            pl.BlockSpec((tm, d), lambda i: (i, 0)),
            pl.BlockSpec((1, d), lambda i: (0, 0)),
            _resident((d, d), lambda i: (0, 0)),
            pl.BlockSpec((tm, dp), lambda i: (i, 0)),
            _resident((dp, d), lambda i: (0, 0)),
        ],
        out_specs=pl.BlockSpec((tm, d), lambda i: (i, 0)),
        out_shape=jax.ShapeDtypeStruct((t, d), F32),
        compiler_params=_compiler_params(1),
        name="ple",
    )(h, g, w_gate, p, w_proj)


def _attn_kernel(slopes_ref, q_ref, k_ref, vt_ref, lq1_ref, lk1_ref, lq2_ref, lk2_ref,
                 gsub_ref, o_ref, bias_ref, dbias_ref, st0_ref, st1_ref, m_ref, l_ref, acc_ref,
                 *, lam0):
    tq = q_ref.shape[1]
    tk = vt_ref.shape[2]
    head = pl.program_id(1)
    i = pl.program_id(2)
    slope = slopes_ref[head] * LOG2E

    @pl.when(i == 0)
    def _():
        row = lax.broadcasted_iota(jnp.int32, (tk, tq), 0)
        col = lax.broadcasted_iota(jnp.int32, (tk, tq), 1)
        bias = slope * row.astype(F32)
        bias_ref[...] = bias
        dbias_ref[...] = jnp.where(row <= col, bias, -jnp.inf)

    m_ref[...] = jnp.full(m_ref.shape, -jnp.inf, F32)
    l_ref[...] = jnp.zeros(l_ref.shape, F32)
    acc_ref[...] = jnp.zeros(acc_ref.shape, F32)

    q = q_ref[0]
    st_refs = (st0_ref, st1_ref)

    def scores(j, c):
        start = pl.multiple_of(j * tk, tk)
        cs = slice(c * HEAD_DIM_QK, (c + 1) * HEAD_DIM_QK)
        st_refs[c][...] = lax.dot_general(
            k_ref[0, pl.ds(start, tk), cs], q[:, cs], (((1,), (1,)), ((), ())),
            preferred_element_type=F32)

    def softmax_pv(j, c, b_ref):
        off = slope * ((j - i) * tk).astype(F32)
        st = st_refs[c][...] + b_ref[...]
        m_prev = m_ref[c]
        m_new = jnp.maximum(m_prev, jnp.max(st, axis=0, keepdims=True) + off)
        alpha = jnp.exp2(m_prev - m_new)
        p = jnp.exp2(st - (m_new - off))
        l_ref[c] = alpha * l_ref[c] + jnp.sum(p, axis=0, keepdims=True)
        acc_ref[c] = alpha * acc_ref[c] + jnp.dot(
            vt_ref[j], p.astype(BF16), preferred_element_type=F32)
        m_ref[c] = m_new

    scores(0, 0)

    def full_block(j):
        scores(j, 1)
        softmax_pv(j, 0, bias_ref)
        scores(j + 1, 0)
        softmax_pv(j, 1, bias_ref)

    def block_pair(jj, carry):
        full_block(2 * jj)
        full_block(2 * jj + 1)
        return carry

    lax.fori_loop(0, lax.shift_right_logical(i, 1), block_pair, 0)

    @pl.when((i & 1) == 1)
    def _():
        full_block(i - 1)

    scores(i, 1)
    softmax_pv(i, 0, dbias_ref)
    softmax_pv(i, 1, dbias_ref)

    lam = (jnp.exp(jnp.sum(lq1_ref[...] * lk1_ref[...], axis=-1, keepdims=True))
           - jnp.exp(jnp.sum(lq2_ref[...] * lk2_ref[...], axis=-1, keepdims=True))
           + lam0)
    ot = acc_ref[0] * (1.0 / l_ref[0]) - lam * (acc_ref[1] * (1.0 / l_ref[1]))
    ms = jnp.mean(ot * ot, axis=0, keepdims=True)
    ot = ot * lax.rsqrt(ms + EPS) * (gsub_ref[...] * (1.0 - lam0))
    o_ref[0] = ot.astype(o_ref.dtype).T


def _diff_attention(q, k, vt, slopes, lam_q1, lam_k1, lam_q2, lam_k2, g_sub, lam0, batch, seq):
    tq = tk = T_ATTN
    nh = N_HEADS
    blk = 2 * HEAD_DIM_QK
    nkv = seq // tk
    vec = lambda: pl.BlockSpec((1, HEAD_DIM_QK), lambda b, h, i: (0, 0))
    kern = functools.partial(_attn_kernel, lam0=lam0)
    return pl.pallas_call(
        kern,
        grid=(batch, nh, seq // tq),
        in_specs=[
            pl.BlockSpec(memory_space=pltpu.SMEM),
            pl.BlockSpec((1, tq, blk), lambda b, h, i: (b, i, h)),
            pl.BlockSpec((1, seq, blk), lambda b, h, i: (b, 0, h)),
            pl.BlockSpec((nkv, HEAD_DIM_V, tk), lambda b, h, i: (b, h, 0)),
            vec(), vec(), vec(), vec(),
            pl.BlockSpec((HEAD_DIM_V, 1), lambda b, h, i: (0, 0)),
        ],
        out_specs=pl.BlockSpec((1, tq, HEAD_DIM_V), lambda b, h, i: (b, i, h)),
        out_shape=jax.ShapeDtypeStruct((batch, seq, nh * HEAD_DIM_V), BF16),
        scratch_shapes=[
            pltpu.VMEM((tk, tq), F32),
            pltpu.VMEM((tk, tq), F32),
            pltpu.VMEM((tk, tq), F32),
            pltpu.VMEM((tk, tq), F32),
            pltpu.VMEM((2, 1, tq), F32),
            pltpu.VMEM((2, 1, tq), F32),
            pltpu.VMEM((2, HEAD_DIM_V, tq), F32),
        ],
        compiler_params=_compiler_params(3),
        name="diff_attention",
    )(slopes, q, k, vt, lam_q1, lam_k1, lam_q2, lam_k2, g_sub)


def _rglru_kernel(gate_ref, xr_ref, cw_ref, cb_ref, wga_ref, bga_ref, wgx_ref, bgx_ref,
                  lam_ref, o_ref, xbuf_ref, a_ref, b_ref, hcar_ref):
    ts = xr_ref.shape[1]
    halo = SUBLANES

    @pl.when(pl.program_id(1) == 0)
    def _():
        xbuf_ref[0:halo, :] = jnp.zeros((halo, D_RNN), F32)
        hcar_ref[...] = jnp.zeros(hcar_ref.shape, F32)

    xbuf_ref[halo:halo + ts, :] = xr_ref[0]
    cw = cw_ref[...]
    xc = cb_ref[...] + cw[CONV_WIDTH - 1:CONV_WIDTH, :] * xbuf_ref[halo:halo + ts, :]
    for back in range(1, CONV_WIDTH):
        w_row = cw[CONV_WIDTH - 1 - back:CONV_WIDTH - back, :]
        xc = xc + w_row * xbuf_ref[halo - back:halo - back + ts, :]
    xbuf_ref[0:halo, :] = xbuf_ref[ts:ts + halo, :]

    neg_c_softplus = -RG_C * jax.nn.softplus(-lam_ref[...])
    for n in range(N_RNN_BLOCKS):
        sl = slice(n * RNN_BLOCK, (n + 1) * RNN_BLOCK)
        xcn = xc[:, sl]
        xb = xcn.astype(BF16)
        r = jax.nn.sigmoid(
            jnp.dot(xb, wga_ref[n], preferred_element_type=F32) + bga_ref[:, sl])
        ig = jax.nn.sigmoid(
            jnp.dot(xb, wgx_ref[n], preferred_element_type=F32) + bgx_ref[:, sl])
        log_a = neg_c_softplus[:, sl] * r
        a = jnp.exp(log_a)
        one_minus_a2 = -jnp.tanh(log_a) * (a * a + 1.0)
        a_ref[:, sl] = a
        b_ref[:, sl] = jnp.sqrt(one_minus_a2) * (ig * xcn)

    sub = lax.broadcasted_iota(jnp.int32, (SUBLANES, SCAN_COLS), 0)
    for cblk in range(D_RNN // SCAN_COLS):
        cs = slice(cblk * SCAN_COLS, (cblk + 1) * SCAN_COLS)

        def tile_step(t, h_prev, cs=cs):
            r0 = pl.multiple_of(t * SUBLANES, SUBLANES)
            a = a_ref[pl.ds(r0, SUBLANES), cs]
            b = b_ref[pl.ds(r0, SUBLANES), cs]
            for d in (1, 2, 4):
                keep = sub >= d
                a_sh = jnp.where(keep, pltpu.roll(a, d, axis=0), 1.0)
                b_sh = jnp.where(keep, pltpu.roll(b, d, axis=0), 0.0)
                b = a * b_sh + b
                a = a * a_sh
            h = a * h_prev + b
            b_ref[pl.ds(r0, SUBLANES), cs] = h
            return jnp.broadcast_to(h[SUBLANES - 1:SUBLANES, :], (SUBLANES, SCAN_COLS))

        hcar_ref[:, cs] = lax.fori_loop(0, ts // SUBLANES, tile_step, hcar_ref[:, cs],
                                        unroll=4)

    y = jax.nn.gelu(gate_ref[0], approximate=True)
    o_ref[0] = (b_ref[...] * y).astype(o_ref.dtype)


def _rglru(u, conv_w, conv_b, w_ga, b_ga, w_gx, b_gx, lam, batch, seq):
    ts = TS_REC
    row = lambda: pl.BlockSpec((1, D_RNN), lambda b, i: (0, 0))
    wblk = lambda: pl.BlockSpec((N_RNN_BLOCKS, RNN_BLOCK, RNN_BLOCK), lambda b, i: (0, 0, 0))
    return pl.pallas_call(
        _rglru_kernel,
        grid=(batch, seq // ts),
        in_specs=[
            pl.BlockSpec((1, ts, D_RNN), lambda b, i: (b, i, 0)),
            pl.BlockSpec((1, ts, D_RNN), lambda b, i: (b, i, 1)),
            pl.BlockSpec((CONV_WIDTH, D_RNN), lambda b, i: (0, 0)),
            row(), wblk(), row(), wblk(), row(), row(),
        ],
        out_specs=pl.BlockSpec((1, ts, D_RNN), lambda b, i: (b, i, 0)),
        out_shape=jax.ShapeDtypeStruct((batch, seq, D_RNN), BF16),
        scratch_shapes=[
            pltpu.VMEM((ts + SUBLANES, D_RNN), F32),
            pltpu.VMEM((ts, D_RNN), F32),
            pltpu.VMEM((ts, D_RNN), F32),
            pltpu.VMEM((SUBLANES, D_RNN), F32),
        ],
        compiler_params=_compiler_params(2),
        name="rglru",
    )(u, u, conv_w, conv_b, w_ga, b_ga, w_gx, b_gx, lam)


def _lambda_init(layer_idx):
    return 0.8 - 0.6 * math.exp(-0.3 * layer_idx)


def kernel(x, p, g_mix, g_mlp, g_ple, w_qkv, g_q, g_k, lam_q1, lam_k1, lam_q2, lam_k2,
           g_sub, w_o_attn, w_in_rec, conv_w, conv_b, w_gate_a, b_gate_a, w_gate_x,
           b_gate_x, lam_rec, w_o_rec, w_up, w_down, w_ple_proj, w_ple_gate):
    batch, seq, d = x.shape
    depth = p.shape[0]
    t = batch * seq
    h = x.reshape(t, d)
    row = lambda v: v.reshape(1, -1)
    slopes = jnp.exp2(-8.0 * jnp.arange(1, N_HEADS + 1, dtype=F32) / N_HEADS)
    n_qk = 2 * N_HEADS * 2 * HEAD_DIM_QK

    for i in range(depth):
        j = i // 2
        if i % 2 == 0:
            w_qk = w_qkv[j][:, :n_qk].astype(BF16)
            w_vt = w_qkv[j][:, n_qk:].T.astype(BF16)
            q, k, vt = _qkv_proj(h, row(g_mix[i]), w_qk, w_vt, row(g_q[j]), row(g_k[j]))
            o = _diff_attention(
                q.reshape(batch, seq, -1), k.reshape(batch, seq, -1), vt, slopes,
                row(lam_q1[j]), row(lam_k1[j]), row(lam_q2[j]), row(lam_k2[j]),
                g_sub[j].reshape(-1, 1), _lambda_init(i), batch, seq)
            h = _matmul_residual(o.reshape(t, -1), w_o_attn[j].astype(BF16), h)
        else:
            u = _norm_matmul(h, row(g_mix[i]), w_in_rec[j].astype(BF16), F32)
            hy = _rglru(
                u.reshape(batch, seq, -1), conv_w[j], row(conv_b[j]),
                w_gate_a[j].astype(BF16), row(b_gate_a[j]), w_gate_x[j].astype(BF16),
                row(b_gate_x[j]), row(lam_rec[j]), batch, seq)
            h = _matmul_residual(hy.reshape(t, -1), w_o_rec[j].astype(BF16), h)
        h = _mlp(h, row(g_mlp[i]), w_up[i].astype(BF16), w_down[i].astype(BF16))
        h = _ple(h, row(g_ple[i]), w_ple_gate[i].astype(BF16), p[i].reshape(t, -1),
                 w_ple_proj[i].astype(BF16))
    return h.reshape(batch, seq, d)
```

```python
import functools
import math

import jax
import jax.numpy as jnp
from jax import lax
from jax.experimental import pallas as pl
from jax.experimental.pallas import tpu as pltpu

D_MODEL = 2048
N_HEADS = 8
HEAD_DIM_QK = 128
HEAD_DIM_V = 256
D_RNN = D_MODEL
N_RNN_BLOCKS = 8
RNN_BLOCK = D_RNN // N_RNN_BLOCKS
CONV_WIDTH = 4
RG_C = 8.0
D_FF = 4 * D_MODEL
D_PLE = 256
EPS = 1e-6
LOG2E = math.log2(math.e)

F32 = jnp.float32
BF16 = jnp.bfloat16

SUBLANES = 8
LANES = 128
VMEM_LIMIT_BYTES = 56 * 1024 * 1024

TM_PROJ = 1024
TN_PROJ = 1024
TM_RES = 512
TN_CHUNK = 512
TM_MLP = 1024
TF_MLP = 512
T_ATTN = 512
TS_REC = 512
SCAN_COLS = 512


def _compiler_params(n_axes):
    return pltpu.CompilerParams(
        dimension_semantics=("arbitrary",) * n_axes,
        vmem_limit_bytes=VMEM_LIMIT_BYTES)


def _resident(block_shape, index_map):
    return pl.BlockSpec(block_shape, index_map, pipeline_mode=pl.Buffered(1))


def _rmsnorm_rows(x, g):
    ms = jnp.mean(x * x, axis=-1, keepdims=True)
    return x * lax.rsqrt(ms + EPS) * g


def _norm_matmul_kernel(h_ref, g_ref, w_ref, o_ref, xn_ref):
    @pl.when(pl.program_id(1) == 0)
    def _():
        xn_ref[...] = _rmsnorm_rows(h_ref[...], g_ref[...]).astype(BF16)

    o_ref[...] = jnp.dot(xn_ref[...], w_ref[...],
                         preferred_element_type=F32).astype(o_ref.dtype)


def _norm_matmul(h, g, w, out_dtype):
    t, d = h.shape
    n = w.shape[1]
    tm, tn = TM_PROJ, TN_PROJ
    return pl.pallas_call(
        _norm_matmul_kernel,
        grid=(t // tm, n // tn),
        in_specs=[
            pl.BlockSpec((tm, d), lambda i, j: (i, 0)),
            pl.BlockSpec((1, d), lambda i, j: (0, 0)),
            pl.BlockSpec((d, tn), lambda i, j: (0, j)),
        ],
        out_specs=pl.BlockSpec((tm, tn), lambda i, j: (i, j)),
        out_shape=jax.ShapeDtypeStruct((t, n), out_dtype),
        scratch_shapes=[pltpu.VMEM((tm, d), BF16)],
        compiler_params=_compiler_params(2),
        name="norm_matmul",
    )(h, g, w)


def _qkv_kernel(h_ref, g_ref, w_ref, gq_ref, gk_ref, q_ref, k_ref, vt_ref, xn_ref,
                *, n_q_tiles, n_qk_tiles):
    j = pl.program_id(1)

    @pl.when(j == 0)
    def _():
        xn_ref[...] = _rmsnorm_rows(h_ref[...], g_ref[...]).astype(BF16)

    def qk_norm_to(out_ref, gain):
        acc = jnp.dot(xn_ref[...], w_ref[...], preferred_element_type=F32)
        for c in range(acc.shape[1] // HEAD_DIM_QK):
            cs = slice(c * HEAD_DIM_QK, (c + 1) * HEAD_DIM_QK)
            blk = acc[:, cs]
            ms = jnp.mean(blk * blk, axis=-1, keepdims=True)
            out_ref[:, cs] = (blk * lax.rsqrt(ms + EPS) * gain).astype(out_ref.dtype)

    @pl.when(j < n_q_tiles)
    def _():
        qk_norm_to(q_ref, gq_ref[...] * (HEAD_DIM_QK ** -0.5 * LOG2E))

    @pl.when(jnp.logical_and(j >= n_q_tiles, j < n_qk_tiles))
    def _():
        qk_norm_to(k_ref, gk_ref[...])

    @pl.when(j >= n_qk_tiles)
    def _():
        tk = vt_ref.shape[2]
        for c in range(w_ref.shape[1] // HEAD_DIM_V):
            cs = slice(c * HEAD_DIM_V, (c + 1) * HEAD_DIM_V)
            v = jnp.dot(xn_ref[...], w_ref[:, cs], preferred_element_type=F32)
            vt = v.astype(vt_ref.dtype).T
            for kb in range(vt_ref.shape[0]):
                vt_ref[kb, cs, :] = vt[:, kb * tk:(kb + 1) * tk]


def _qkv_proj(h, g, w, layer, g_q, g_k):
    t, d = h.shape
    n = w.shape[2]
    nv = N_HEADS * HEAD_DIM_V
    nq = (n - nv) // 2
    tm, tn = TM_PROJ, TN_PROJ
    tk = T_ATTN
    n_q_tiles = nq // tn
    n_qk_tiles = 2 * n_q_tiles
    kern = functools.partial(_qkv_kernel, n_q_tiles=n_q_tiles, n_qk_tiles=n_qk_tiles)
    clamp = lambda v, lo, hi: jnp.minimum(jnp.maximum(v, lo), hi)
    return pl.pallas_call(
        kern,
        grid=(t // tm, n // tn),
        in_specs=[
            pl.BlockSpec((tm, d), lambda i, j: (i, 0)),
            pl.BlockSpec((1, d), lambda i, j: (0, 0)),
            pl.BlockSpec((None, d, tn), lambda i, j: (layer, 0, j)),
            pl.BlockSpec((1, HEAD_DIM_QK), lambda i, j: (0, 0)),
            pl.BlockSpec((1, HEAD_DIM_QK), lambda i, j: (0, 0)),
        ],
        out_specs=[
            pl.BlockSpec((tm, tn), lambda i, j: (i, jnp.minimum(j, n_q_tiles - 1))),
            pl.BlockSpec((tm, tn), lambda i, j: (i, clamp(j - n_q_tiles, 0, n_q_tiles - 1))),
            pl.BlockSpec((tm // tk, tn, tk),
                         lambda i, j: (i, clamp(j - n_qk_tiles, 0, nv // tn - 1), 0)),
        ],
        out_shape=[
            jax.ShapeDtypeStruct((t, nq), BF16),
            jax.ShapeDtypeStruct((t, nq), BF16),
            jax.ShapeDtypeStruct((t // tk, nv, tk), BF16),
        ],
        scratch_shapes=[pltpu.VMEM((tm, d), BF16)],
        compiler_params=_compiler_params(2),
        name="qkv_proj",
    )(h, g, w, g_q, g_k)


def _matmul_residual_kernel(a_ref, w_ref, h_ref, o_ref):
    a = a_ref[...]
    for c in range(o_ref.shape[1] // TN_CHUNK):
        cs = slice(c * TN_CHUNK, (c + 1) * TN_CHUNK)
        o_ref[:, cs] = h_ref[:, cs] + jnp.dot(a, w_ref[:, cs], preferred_element_type=F32)


def _matmul_residual(a, w, h):
    t, k = a.shape
    n = w.shape[1]
    tm = TM_RES
    return pl.pallas_call(
        _matmul_residual_kernel,
        grid=(t // tm,),
        in_specs=[
            pl.BlockSpec((tm, k), lambda i: (i, 0)),
            _resident((k, n), lambda i: (0, 0)),
            pl.BlockSpec((tm, n), lambda i: (i, 0)),
        ],
        out_specs=pl.BlockSpec((tm, n), lambda i: (i, 0)),
        out_shape=jax.ShapeDtypeStruct((t, n), F32),
        compiler_params=_compiler_params(1),
        name="matmul_residual",
    )(a, w, h)


def _mlp_kernel(h_ref, g_ref, wu_ref, wd_ref, o_ref, xn_ref):
    @pl.when(pl.program_id(1) == 0)
    def _():
        x = h_ref[...]
        xn_ref[...] = _rmsnorm_rows(x, g_ref[...]).astype(BF16)
        o_ref[...] = x

    u = jnp.dot(xn_ref[...], wu_ref[...], preferred_element_type=F32)
    a = jnp.square(jnp.maximum(u, 0.0)).astype(BF16)
    o_ref[...] += jnp.dot(a, wd_ref[...], preferred_element_type=F32)


def _mlp(h, g, w_up, w_down, layer):
    t, d = h.shape
    f = w_up.shape[2]
    tm, tf = TM_MLP, TF_MLP
    return pl.pallas_call(
        _mlp_kernel,
        grid=(t // tm, f // tf),
        in_specs=[
            pl.BlockSpec((tm, d), lambda i, j: (i, 0)),
            pl.BlockSpec((1, d), lambda i, j: (0, 0)),
            pl.BlockSpec((None, d, tf), lambda i, j: (layer, 0, j)),
            pl.BlockSpec((None, tf, d), lambda i, j: (layer, j, 0)),
        ],
        out_specs=pl.BlockSpec((tm, d), lambda i, j: (i, 0)),
        out_shape=jax.ShapeDtypeStruct((t, d), F32),
        scratch_shapes=[pltpu.VMEM((tm, d), BF16)],
        compiler_params=_compiler_params(2),
        name="mlp",
    )(h, g, w_up, w_down)


def _ple_kernel(h_ref, g_ref, wg_ref, p_ref, wp_ref, o_ref):
    xn = _rmsnorm_rows(h_ref[...], g_ref[...]).astype(BF16)
    pb = p_ref[...].astype(BF16)
    for c in range(o_ref.shape[1] // TN_CHUNK):
        cs = slice(c * TN_CHUNK, (c + 1) * TN_CHUNK)
        gate = jax.nn.sigmoid(jnp.dot(xn, wg_ref[:, cs], preferred_element_type=F32))
        proj = jnp.dot(pb, wp_ref[:, cs], preferred_element_type=F32)
        o_ref[:, cs] = h_ref[:, cs] + gate * proj


def _ple(h, g, w_gate, p, w_proj, layer):
    t, d = h.shape
    dp = p.shape[2]
    tm = TM_RES
    return pl.pallas_call(
        _ple_kernel,
        grid=(t // tm,),
        in_specs=[
            pl.BlockSpec((tm, d), lambda i: (i, 0)),
            pl.BlockSpec((1, d), lambda i: (0, 0)),
            _resident((None, d, d), lambda i: (layer, 0, 0)),
            pl.BlockSpec((None, tm, dp), lambda i: (layer, i, 0)),
            _resident((None, dp, d), lambda i: (layer, 0, 0)),
        ],
        out_specs=pl.BlockSpec((tm, d), lambda i: (i, 0)),
        out_shape=jax.ShapeDtypeStruct((t, d), F32),
        compiler_params=_compiler_params(1),
        name="ple",
    )(h, g, w_gate, p, w_proj)


def _attn_kernel(slopes_ref, q_ref, k_ref, vt_ref, lq1_ref, lk1_ref, lq2_ref, lk2_ref,
                 gsub_ref, o_ref, bias_ref, dbias_ref, st0_ref, st1_ref, m_ref, l_ref, acc_ref,
                 *, lam0):
    tq = q_ref.shape[1]
    tk = vt_ref.shape[2]
    head = pl.program_id(1)
    i = pl.program_id(2)
    slope = slopes_ref[head] * LOG2E

    @pl.when(i == 0)
    def _():
        row = lax.broadcasted_iota(jnp.int32, (tk, tq), 0)
        col = lax.broadcasted_iota(jnp.int32, (tk, tq), 1)
        bias = slope * row.astype(F32)
        bias_ref[...] = bias
        dbias_ref[...] = jnp.where(row <= col, bias, -jnp.inf)

    m_ref[...] = jnp.full(m_ref.shape, -jnp.inf, F32)
    l_ref[...] = jnp.zeros(l_ref.shape, F32)
    acc_ref[...] = jnp.zeros(acc_ref.shape, F32)

    q = q_ref[0]
    st_refs = (st0_ref, st1_ref)

    def scores(j, c):
        start = pl.multiple_of(j * tk, tk)
        cs = slice(c * HEAD_DIM_QK, (c + 1) * HEAD_DIM_QK)
        st_refs[c][...] = lax.dot_general(
            k_ref[0, pl.ds(start, tk), cs], q[:, cs], (((1,), (1,)), ((), ())),
            preferred_element_type=F32)

    def softmax_pv(j, c, b_ref):
        off = slope * ((j - i) * tk).astype(F32)
        st = st_refs[c][...] + b_ref[...]
        m_prev = m_ref[c]
        m_new = jnp.maximum(m_prev, jnp.max(st, axis=0, keepdims=True) + off)
        alpha = jnp.exp2(m_prev - m_new)
        p = jnp.exp2(st - (m_new - off))
        l_ref[c] = alpha * l_ref[c] + jnp.sum(p, axis=0, keepdims=True)
        acc_ref[c] = alpha * acc_ref[c] + jnp.dot(
            vt_ref[j], p.astype(BF16), preferred_element_type=F32)
        m_ref[c] = m_new

    scores(0, 0)

    def full_block(j):
        scores(j, 1)
        softmax_pv(j, 0, bias_ref)
        scores(j + 1, 0)
        softmax_pv(j, 1, bias_ref)

    def block_pair(jj, carry):
        full_block(2 * jj)
        full_block(2 * jj + 1)
        return carry

    lax.fori_loop(0, lax.shift_right_logical(i, 1), block_pair, 0)

    @pl.when((i & 1) == 1)
    def _():
        full_block(i - 1)

    scores(i, 1)
    softmax_pv(i, 0, dbias_ref)
    softmax_pv(i, 1, dbias_ref)

    lam = (jnp.exp(jnp.sum(lq1_ref[...] * lk1_ref[...], axis=-1, keepdims=True))
           - jnp.exp(jnp.sum(lq2_ref[...] * lk2_ref[...], axis=-1, keepdims=True))
           + lam0)
    ot = acc_ref[0] * (1.0 / l_ref[0]) - lam * (acc_ref[1] * (1.0 / l_ref[1]))
    ms = jnp.mean(ot * ot, axis=0, keepdims=True)
    ot = ot * lax.rsqrt(ms + EPS) * (gsub_ref[...] * (1.0 - lam0))
    o_ref[0] = ot.astype(o_ref.dtype).T


def _diff_attention(q, k, vt, slopes, lam_q1, lam_k1, lam_q2, lam_k2, g_sub, lam0, batch, seq):
    tq = tk = T_ATTN
    nh = N_HEADS
    blk = 2 * HEAD_DIM_QK
    nkv = seq // tk
    vec = lambda: pl.BlockSpec((1, HEAD_DIM_QK), lambda b, h, i: (0, 0))
    kern = functools.partial(_attn_kernel, lam0=lam0)
    return pl.pallas_call(
        kern,
        grid=(batch, nh, seq // tq),
        in_specs=[
            pl.BlockSpec(memory_space=pltpu.SMEM),
            pl.BlockSpec((1, tq, blk), lambda b, h, i: (b, i, h)),
            pl.BlockSpec((1, seq, blk), lambda b, h, i: (b, 0, h)),
            pl.BlockSpec((nkv, HEAD_DIM_V, tk), lambda b, h, i: (b, h, 0)),
            vec(), vec(), vec(), vec(),
            pl.BlockSpec((HEAD_DIM_V, 1), lambda b, h, i: (0, 0)),
        ],
        out_specs=pl.BlockSpec((1, tq, HEAD_DIM_V), lambda b, h, i: (b, i, h)),
        out_shape=jax.ShapeDtypeStruct((batch, seq, nh * HEAD_DIM_V), BF16),
        scratch_shapes=[
            pltpu.VMEM((tk, tq), F32),
            pltpu.VMEM((tk, tq), F32),
            pltpu.VMEM((tk, tq), F32),
            pltpu.VMEM((tk, tq), F32),
            pltpu.VMEM((2, 1, tq), F32),
            pltpu.VMEM((2, 1, tq), F32),
            pltpu.VMEM((2, HEAD_DIM_V, tq), F32),
        ],
        compiler_params=_compiler_params(3),
        name="diff_attention",
    )(slopes, q, k, vt, lam_q1, lam_k1, lam_q2, lam_k2, g_sub)


def _rglru_kernel(gate_ref, xr_ref, cw_ref, cb_ref, wga_ref, bga_ref, wgx_ref, bgx_ref,
                  lam_ref, o_ref, xbuf_ref, a_ref, b_ref, hcar_ref):
    ts = xr_ref.shape[1]
    halo = SUBLANES

    @pl.when(pl.program_id(1) == 0)
    def _():
        xbuf_ref[0:halo, :] = jnp.zeros((halo, D_RNN), F32)
        hcar_ref[...] = jnp.zeros(hcar_ref.shape, F32)

    xbuf_ref[halo:halo + ts, :] = xr_ref[0]
    cw = cw_ref[...]
    xc = cb_ref[...] + cw[CONV_WIDTH - 1:CONV_WIDTH, :] * xbuf_ref[halo:halo + ts, :]
    for back in range(1, CONV_WIDTH):
        w_row = cw[CONV_WIDTH - 1 - back:CONV_WIDTH - back, :]
        xc = xc + w_row * xbuf_ref[halo - back:halo - back + ts, :]
    xbuf_ref[0:halo, :] = xbuf_ref[ts:ts + halo, :]

    neg_c_softplus = -RG_C * jax.nn.softplus(-lam_ref[...])
    for n in range(N_RNN_BLOCKS):
        sl = slice(n * RNN_BLOCK, (n + 1) * RNN_BLOCK)
        xcn = xc[:, sl]
        xb = xcn.astype(BF16)
        r = jax.nn.sigmoid(
            jnp.dot(xb, wga_ref[n], preferred_element_type=F32) + bga_ref[:, sl])
        ig = jax.nn.sigmoid(
            jnp.dot(xb, wgx_ref[n], preferred_element_type=F32) + bgx_ref[:, sl])
        log_a = neg_c_softplus[:, sl] * r
        a = jnp.exp(log_a)
        one_minus_a2 = -jnp.tanh(log_a) * (a * a + 1.0)
        a_ref[:, sl] = a
        b_ref[:, sl] = jnp.sqrt(one_minus_a2) * (ig * xcn)

    sub = lax.broadcasted_iota(jnp.int32, (SUBLANES, SCAN_COLS), 0)
    for cblk in range(D_RNN // SCAN_COLS):
        cs = slice(cblk * SCAN_COLS, (cblk + 1) * SCAN_COLS)

        def tile_step(t, h_prev, cs=cs):
            r0 = pl.multiple_of(t * SUBLANES, SUBLANES)
            a = a_ref[pl.ds(r0, SUBLANES), cs]
            b = b_ref[pl.ds(r0, SUBLANES), cs]
            for d in (1, 2, 4):
                keep = sub >= d
                a_sh = jnp.where(keep, pltpu.roll(a, d, axis=0), 1.0)
                b_sh = jnp.where(keep, pltpu.roll(b, d, axis=0), 0.0)
                b = a * b_sh + b
                a = a * a_sh
            h = a * h_prev + b
            b_ref[pl.ds(r0, SUBLANES), cs] = h
            return jnp.broadcast_to(h[SUBLANES - 1:SUBLANES, :], (SUBLANES, SCAN_COLS))

        hcar_ref[:, cs] = lax.fori_loop(0, ts // SUBLANES, tile_step, hcar_ref[:, cs],
                                        unroll=4)

    y = jax.nn.gelu(gate_ref[0], approximate=True)
    o_ref[0] = (b_ref[...] * y).astype(o_ref.dtype)


def _rglru(u, conv_w, conv_b, w_ga, b_ga, w_gx, b_gx, lam, batch, seq):
    ts = TS_REC
    row = lambda: pl.BlockSpec((1, D_RNN), lambda b, i: (0, 0))
    wblk = lambda: pl.BlockSpec((N_RNN_BLOCKS, RNN_BLOCK, RNN_BLOCK), lambda b, i: (0, 0, 0))
    return pl.pallas_call(
        _rglru_kernel,
        grid=(batch, seq // ts),
        in_specs=[
            pl.BlockSpec((1, ts, D_RNN), lambda b, i: (b, i, 0)),
            pl.BlockSpec((1, ts, D_RNN), lambda b, i: (b, i, 1)),
            pl.BlockSpec((CONV_WIDTH, D_RNN), lambda b, i: (0, 0)),
            row(), wblk(), row(), wblk(), row(), row(),
        ],
        out_specs=pl.BlockSpec((1, ts, D_RNN), lambda b, i: (b, i, 0)),
        out_shape=jax.ShapeDtypeStruct((batch, seq, D_RNN), BF16),
        scratch_shapes=[
            pltpu.VMEM((ts + SUBLANES, D_RNN), F32),
            pltpu.VMEM((ts, D_RNN), F32),
            pltpu.VMEM((ts, D_RNN), F32),
            pltpu.VMEM((SUBLANES, D_RNN), F32),
        ],
        compiler_params=_compiler_params(2),
        name="rglru",
    )(u, u, conv_w, conv_b, w_ga, b_ga, w_gx, b_gx, lam)


def _lambda_init(layer_idx):
    return 0.8 - 0.6 * math.exp(-0.3 * layer_idx)


def kernel(x, p, g_mix, g_mlp, g_ple, w_qkv, g_q, g_k, lam_q1, lam_k1, lam_q2, lam_k2,
           g_sub, w_o_attn, w_in_rec, conv_w, conv_b, w_gate_a, b_gate_a, w_gate_x,
           b_gate_x, lam_rec, w_o_rec, w_up, w_down, w_ple_proj, w_ple_gate):
    batch, seq, d = x.shape
    depth = p.shape[0]
    t = batch * seq
    h = x.reshape(t, d)
    row = lambda v: v.reshape(1, -1)
    slopes = jnp.exp2(-8.0 * jnp.arange(1, N_HEADS + 1, dtype=F32) / N_HEADS)
    w_qkv_b = w_qkv.astype(BF16)
    w_up_b = w_up.astype(BF16)
    w_down_b = w_down.astype(BF16)
    w_ple_gate_b = w_ple_gate.astype(BF16)
    w_ple_proj_b = w_ple_proj.astype(BF16)
    p_rows = p.reshape(depth, t, -1)

    for i in range(depth):
        j = i // 2
        if i % 2 == 0:
            q, k, vt = _qkv_proj(h, row(g_mix[i]), w_qkv_b, j, row(g_q[j]), row(g_k[j]))
            o = _diff_attention(
                q.reshape(batch, seq, -1), k.reshape(batch, seq, -1), vt, slopes,
                row(lam_q1[j]), row(lam_k1[j]), row(lam_q2[j]), row(lam_k2[j]),
                g_sub[j].reshape(-1, 1), _lambda_init(i), batch, seq)
            h = _matmul_residual(o.reshape(t, -1), w_o_attn[j].astype(BF16), h)
        else:
            u = _norm_matmul(h, row(g_mix[i]), w_in_rec[j].astype(BF16), F32)
            hy = _rglru(
                u.reshape(batch, seq, -1), conv_w[j], row(conv_b[j]),
                w_gate_a[j].astype(BF16), row(b_gate_a[j]), w_gate_x[j].astype(BF16),
                row(b_gate_x[j]), row(lam_rec[j]), batch, seq)
            h = _matmul_residual(hy.reshape(t, -1), w_o_rec[j].astype(BF16), h)
        h = _mlp(h, row(g_mlp[i]), w_up_b, w_down_b, i)
        h = _ple(h, row(g_ple[i]), w_ple_gate_b, p_rows, w_ple_proj_b, i)
    return h.reshape(batch, seq, d)
```

```python
import functools
import math

import jax
import jax.numpy as jnp
from jax import lax
from jax.experimental import pallas as pl
from jax.experimental.pallas import tpu as pltpu

D_MODEL = 2048
N_HEADS = 8
HEAD_DIM_QK = 128
HEAD_DIM_V = 256
D_RNN = D_MODEL
N_RNN_BLOCKS = 8
RNN_BLOCK = D_RNN // N_RNN_BLOCKS
CONV_WIDTH = 4
RG_C = 8.0
D_FF = 4 * D_MODEL
D_PLE = 256
EPS = 1e-6
LOG2E = math.log2(math.e)

F32 = jnp.float32
BF16 = jnp.bfloat16

SUBLANES = 8
LANES = 128
VMEM_LIMIT_BYTES = 56 * 1024 * 1024

TM_PROJ = 1024
TN_PROJ = 1024
TM_RES = 512
TN_CHUNK = 512
TM_MLP = 1024
TF_MLP = 512
T_ATTN = 512
TS_REC = 512
REC_ROW_CHUNKS = 2
SCAN_COLS = 512


def _compiler_params(n_axes):
    return pltpu.CompilerParams(
        dimension_semantics=("arbitrary",) * n_axes,
        vmem_limit_bytes=VMEM_LIMIT_BYTES)


def _resident(block_shape, index_map):
    return pl.BlockSpec(block_shape, index_map, pipeline_mode=pl.Buffered(1))


def _rmsnorm_rows(x, g):
    ms = jnp.mean(x * x, axis=-1, keepdims=True)
    return x * lax.rsqrt(ms + EPS) * g


def _qkv_kernel(h_ref, g_ref, w_ref, gq_ref, gk_ref, q_ref, k_ref, vt_ref, xn_ref,
                *, n_q_tiles, n_qk_tiles):
    j = pl.program_id(1)

    @pl.when(j == 0)
    def _():
        xn_ref[...] = _rmsnorm_rows(h_ref[...], g_ref[...]).astype(BF16)

    def qk_norm_to(out_ref, gain):
        acc = jnp.dot(xn_ref[...], w_ref[...], preferred_element_type=F32)
        for c in range(acc.shape[1] // HEAD_DIM_QK):
            cs = slice(c * HEAD_DIM_QK, (c + 1) * HEAD_DIM_QK)
            blk = acc[:, cs]
            ms = jnp.mean(blk * blk, axis=-1, keepdims=True)
            out_ref[:, cs] = (blk * lax.rsqrt(ms + EPS) * gain).astype(out_ref.dtype)

    @pl.when(j < n_q_tiles)
    def _():
        qk_norm_to(q_ref, gq_ref[...] * (HEAD_DIM_QK ** -0.5 * LOG2E))

    @pl.when(jnp.logical_and(j >= n_q_tiles, j < n_qk_tiles))
    def _():
        qk_norm_to(k_ref, gk_ref[...])

    @pl.when(j >= n_qk_tiles)
    def _():
        tk = vt_ref.shape[2]
        for c in range(w_ref.shape[1] // HEAD_DIM_V):
            cs = slice(c * HEAD_DIM_V, (c + 1) * HEAD_DIM_V)
            v = jnp.dot(xn_ref[...], w_ref[:, cs], preferred_element_type=F32)
            vt = v.astype(vt_ref.dtype).T
            for kb in range(vt_ref.shape[0]):
                vt_ref[kb, cs, :] = vt[:, kb * tk:(kb + 1) * tk]


def _qkv_proj(h, g, w, layer, g_q, g_k):
    t, d = h.shape
    n = w.shape[2]
    nv = N_HEADS * HEAD_DIM_V
    nq = (n - nv) // 2
    tm, tn = TM_PROJ, TN_PROJ
    tk = T_ATTN
    n_q_tiles = nq // tn
    n_qk_tiles = 2 * n_q_tiles
    kern = functools.partial(_qkv_kernel, n_q_tiles=n_q_tiles, n_qk_tiles=n_qk_tiles)
    clamp = lambda v, lo, hi: jnp.minimum(jnp.maximum(v, lo), hi)
    return pl.pallas_call(
        kern,
        grid=(t // tm, n // tn),
        in_specs=[
            pl.BlockSpec((tm, d), lambda i, j: (i, 0)),
            pl.BlockSpec((1, d), lambda i, j: (0, 0)),
            pl.BlockSpec((None, d, tn), lambda i, j: (layer, 0, j)),
            pl.BlockSpec((1, HEAD_DIM_QK), lambda i, j: (0, 0)),
            pl.BlockSpec((1, HEAD_DIM_QK), lambda i, j: (0, 0)),
        ],
        out_specs=[
            pl.BlockSpec((tm, tn), lambda i, j: (i, jnp.minimum(j, n_q_tiles - 1))),
            pl.BlockSpec((tm, tn), lambda i, j: (i, clamp(j - n_q_tiles, 0, n_q_tiles - 1))),
            pl.BlockSpec((tm // tk, tn, tk),
                         lambda i, j: (i, clamp(j - n_qk_tiles, 0, nv // tn - 1), 0)),
        ],
        out_shape=[
            jax.ShapeDtypeStruct((t, nq), BF16),
            jax.ShapeDtypeStruct((t, nq), BF16),
            jax.ShapeDtypeStruct((t // tk, nv, tk), BF16),
        ],
        scratch_shapes=[pltpu.VMEM((tm, d), BF16)],
        compiler_params=_compiler_params(2),
        name="qkv_proj",
    )(h, g, w, g_q, g_k)


def _matmul_residual_kernel(a_ref, w_ref, h_ref, o_ref):
    a = a_ref[...]
    for c in range(o_ref.shape[1] // TN_CHUNK):
        cs = slice(c * TN_CHUNK, (c + 1) * TN_CHUNK)
        o_ref[:, cs] = h_ref[:, cs] + jnp.dot(a, w_ref[:, cs], preferred_element_type=F32)


def _matmul_residual(a, w, h):
    t, k = a.shape
    n = w.shape[1]
    tm = TM_RES
    return pl.pallas_call(
        _matmul_residual_kernel,
        grid=(t // tm,),
        in_specs=[
            pl.BlockSpec((tm, k), lambda i: (i, 0)),
            _resident((k, n), lambda i: (0, 0)),
            pl.BlockSpec((tm, n), lambda i: (i, 0)),
        ],
        out_specs=pl.BlockSpec((tm, n), lambda i: (i, 0)),
        out_shape=jax.ShapeDtypeStruct((t, n), F32),
        compiler_params=_compiler_params(1),
        name="matmul_residual",
    )(a, w, h)


def _mlp_kernel(h_ref, g_ref, wu_ref, wd_ref, o_ref, xn_ref):
    @pl.when(pl.program_id(1) == 0)
    def _():
        x = h_ref[...]
        xn_ref[...] = _rmsnorm_rows(x, g_ref[...]).astype(BF16)
        o_ref[...] = x

    u = jnp.dot(xn_ref[...], wu_ref[...], preferred_element_type=F32)
    a = jnp.square(jnp.maximum(u, 0.0)).astype(BF16)
    o_ref[...] += jnp.dot(a, wd_ref[...], preferred_element_type=F32)


def _mlp(h, g, w_up, w_down, layer):
    t, d = h.shape
    f = w_up.shape[2]
    tm, tf = TM_MLP, TF_MLP
    return pl.pallas_call(
        _mlp_kernel,
        grid=(t // tm, f // tf),
        in_specs=[
            pl.BlockSpec((tm, d), lambda i, j: (i, 0)),
            pl.BlockSpec((1, d), lambda i, j: (0, 0)),
            pl.BlockSpec((None, d, tf), lambda i, j: (layer, 0, j)),
            pl.BlockSpec((None, tf, d), lambda i, j: (layer, j, 0)),
        ],
        out_specs=pl.BlockSpec((tm, d), lambda i, j: (i, 0)),
        out_shape=jax.ShapeDtypeStruct((t, d), F32),
        scratch_shapes=[pltpu.VMEM((tm, d), BF16)],
        compiler_params=_compiler_params(2),
        name="mlp",
    )(h, g, w_up, w_down)


def _ple_kernel(h_ref, g_ref, wg_ref, p_ref, wp_ref, o_ref):
    xn = _rmsnorm_rows(h_ref[...], g_ref[...]).astype(BF16)
    pb = p_ref[...].astype(BF16)
    for c in range(o_ref.shape[1] // TN_CHUNK):
        cs = slice(c * TN_CHUNK, (c + 1) * TN_CHUNK)
        gate = jax.nn.sigmoid(jnp.dot(xn, wg_ref[:, cs], preferred_element_type=F32))
        proj = jnp.dot(pb, wp_ref[:, cs], preferred_element_type=F32)
        o_ref[:, cs] = h_ref[:, cs] + gate * proj


def _ple(h, g, w_gate, p, w_proj, layer):
    t, d = h.shape
    dp = p.shape[2]
    tm = TM_RES
    return pl.pallas_call(
        _ple_kernel,
        grid=(t // tm,),
        in_specs=[
            pl.BlockSpec((tm, d), lambda i: (i, 0)),
            pl.BlockSpec((1, d), lambda i: (0, 0)),
            _resident((None, d, d), lambda i: (layer, 0, 0)),
            pl.BlockSpec((None, tm, dp), lambda i: (layer, i, 0)),
            _resident((None, dp, d), lambda i: (layer, 0, 0)),
        ],
        out_specs=pl.BlockSpec((tm, d), lambda i: (i, 0)),
        out_shape=jax.ShapeDtypeStruct((t, d), F32),
        compiler_params=_compiler_params(1),
        name="ple",
    )(h, g, w_gate, p, w_proj)


def _attn_kernel(slopes_ref, q_ref, k_ref, vt_ref, lq1_ref, lk1_ref, lq2_ref, lk2_ref,
                 gsub_ref, o_ref, bias_ref, dbias_ref, st0_ref, st1_ref, m_ref, l_ref, acc_ref,
                 *, lam0):
    tq = q_ref.shape[1]
    tk = vt_ref.shape[2]
    head = pl.program_id(1)
    i = pl.program_id(2)
    slope = slopes_ref[head] * LOG2E

    @pl.when(i == 0)
    def _():
        row = lax.broadcasted_iota(jnp.int32, (tk, tq), 0)
        col = lax.broadcasted_iota(jnp.int32, (tk, tq), 1)
        bias = slope * row.astype(F32)
        bias_ref[...] = bias
        dbias_ref[...] = jnp.where(row <= col, bias, -jnp.inf)

    m_ref[...] = jnp.full(m_ref.shape, -jnp.inf, F32)
    l_ref[...] = jnp.zeros(l_ref.shape, F32)
    acc_ref[...] = jnp.zeros(acc_ref.shape, F32)

    q = q_ref[0]
    st_refs = (st0_ref, st1_ref)

    def scores(j, c):
        start = pl.multiple_of(j * tk, tk)
        cs = slice(c * HEAD_DIM_QK, (c + 1) * HEAD_DIM_QK)
        st_refs[c][...] = lax.dot_general(
            k_ref[0, pl.ds(start, tk), cs], q[:, cs], (((1,), (1,)), ((), ())),
            preferred_element_type=F32)

    def softmax_pv(j, c, b_ref):
        off = slope * ((j - i) * tk).astype(F32)
        st = st_refs[c][...] + b_ref[...]
        m_prev = m_ref[c]
        m_new = jnp.maximum(m_prev, jnp.max(st, axis=0, keepdims=True) + off)
        alpha = jnp.exp2(m_prev - m_new)
        p = jnp.exp2(st - (m_new - off))
        l_ref[c] = alpha * l_ref[c] + jnp.sum(p, axis=0, keepdims=True)
        acc_ref[c] = alpha * acc_ref[c] + jnp.dot(
            vt_ref[j], p.astype(BF16), preferred_element_type=F32)
        m_ref[c] = m_new

    scores(0, 0)

    def full_block(j):
        scores(j, 1)
        softmax_pv(j, 0, bias_ref)
        scores(j + 1, 0)
        softmax_pv(j, 1, bias_ref)

    def block_pair(jj, carry):
        full_block(2 * jj)
        full_block(2 * jj + 1)
        return carry

    lax.fori_loop(0, lax.shift_right_logical(i, 1), block_pair, 0)

    @pl.when((i & 1) == 1)
    def _():
        full_block(i - 1)

    scores(i, 1)
    softmax_pv(i, 0, dbias_ref)
    softmax_pv(i, 1, dbias_ref)

    lam = (jnp.exp(jnp.sum(lq1_ref[...] * lk1_ref[...], axis=-1, keepdims=True))
           - jnp.exp(jnp.sum(lq2_ref[...] * lk2_ref[...], axis=-1, keepdims=True))
           + lam0)
    ot = acc_ref[0] * (1.0 / l_ref[0]) - lam * (acc_ref[1] * (1.0 / l_ref[1]))
    ms = jnp.mean(ot * ot, axis=0, keepdims=True)
    ot = ot * lax.rsqrt(ms + EPS) * (gsub_ref[...] * (1.0 - lam0))
    o_ref[0] = ot.astype(o_ref.dtype).T


def _diff_attention(q, k, vt, slopes, lam_q1, lam_k1, lam_q2, lam_k2, g_sub, lam0, batch, seq):
    tq = tk = T_ATTN
    nh = N_HEADS
    blk = 2 * HEAD_DIM_QK
    nkv = seq // tk
    vec = lambda: pl.BlockSpec((1, HEAD_DIM_QK), lambda b, h, i: (0, 0))
    kern = functools.partial(_attn_kernel, lam0=lam0)
    return pl.pallas_call(
        kern,
        grid=(batch, nh, seq // tq),
        in_specs=[
            pl.BlockSpec(memory_space=pltpu.SMEM),
            pl.BlockSpec((1, tq, blk), lambda b, h, i: (b, i, h)),
            pl.BlockSpec((1, seq, blk), lambda b, h, i: (b, 0, h)),
            pl.BlockSpec((nkv, HEAD_DIM_V, tk), lambda b, h, i: (b, h, 0)),
            vec(), vec(), vec(), vec(),
            pl.BlockSpec((HEAD_DIM_V, 1), lambda b, h, i: (0, 0)),
        ],
        out_specs=pl.BlockSpec((1, tq, HEAD_DIM_V), lambda b, h, i: (b, i, h)),
        out_shape=jax.ShapeDtypeStruct((batch, seq, nh * HEAD_DIM_V), BF16),
        scratch_shapes=[
            pltpu.VMEM((tk, tq), F32),
            pltpu.VMEM((tk, tq), F32),
            pltpu.VMEM((tk, tq), F32),
            pltpu.VMEM((tk, tq), F32),
            pltpu.VMEM((2, 1, tq), F32),
            pltpu.VMEM((2, 1, tq), F32),
            pltpu.VMEM((2, HEAD_DIM_V, tq), F32),
        ],
        compiler_params=_compiler_params(3),
        name="diff_attention",
    )(slopes, q, k, vt, lam_q1, lam_k1, lam_q2, lam_k2, g_sub)


def _rglru_kernel(h_ref, g_ref, win_ref, cw_ref, cb_ref, wga_ref, bga_ref, wgx_ref, bgx_ref,
                  lam_ref, o_ref, xbuf_ref, gate_ref, a_ref, b_ref, hcar_ref):
    ts = h_ref.shape[1]
    halo = SUBLANES

    @pl.when(pl.program_id(1) == 0)
    def _():
        xbuf_ref[0:halo, :] = jnp.zeros((halo, D_RNN), F32)
        hcar_ref[...] = jnp.zeros(hcar_ref.shape, F32)

    cw = cw_ref[...]
    neg_c_softplus = -RG_C * jax.nn.softplus(-lam_ref[...])
    sub = lax.broadcasted_iota(jnp.int32, (SUBLANES, SCAN_COLS), 0)

    tr = ts // REC_ROW_CHUNKS
    n_pieces = D_RNN // TN_CHUNK
    assert N_RNN_BLOCKS == 2 * n_pieces and D_RNN // SCAN_COLS == n_pieces

    def norm_rows(rc):
        return _rmsnorm_rows(h_ref[0, rc * tr:(rc + 1) * tr, :], g_ref[...]).astype(BF16)

    def recurrent_piece(rc, c, xn):
        xbuf_ref[halo + rc * tr:halo + (rc + 1) * tr, c * TN_CHUNK:(c + 1) * TN_CHUNK] = jnp.dot(
            xn, win_ref[:, D_RNN + c * TN_CHUNK:D_RNN + (c + 1) * TN_CHUNK],
            preferred_element_type=F32)

    def gate_piece(rc, c, xn):
        cs = slice(c * TN_CHUNK, (c + 1) * TN_CHUNK)
        gate_ref[rc * tr:(rc + 1) * tr, cs] = jnp.dot(
            xn, win_ref[:, cs], preferred_element_type=F32)

    xn = norm_rows(0)
    for c in range(n_pieces):
        recurrent_piece(0, c, xn)

    for rc in range(REC_ROW_CHUNKS):
        rows = slice(rc * tr, (rc + 1) * tr)
        xrows = slice(halo + rc * tr, halo + (rc + 1) * tr)
        has_next = rc + 1 < REC_ROW_CHUNKS

        xc = cb_ref[...] + cw[CONV_WIDTH - 1:CONV_WIDTH, :] * xbuf_ref[xrows, :]
        for back in range(1, CONV_WIDTH):
            w_row = cw[CONV_WIDTH - 1 - back:CONV_WIDTH - back, :]
            xc = xc + w_row * xbuf_ref[halo + rc * tr - back:halo + (rc + 1) * tr - back, :]
        xn_next = norm_rows(rc + 1) if has_next else None

        for n in range(N_RNN_BLOCKS):
            sl = slice(n * RNN_BLOCK, (n + 1) * RNN_BLOCK)
            xcn = xc[:, sl]
            xb = xcn.astype(BF16)
            r = jax.nn.sigmoid(
                jnp.dot(xb, wga_ref[n], preferred_element_type=F32) + bga_ref[:, sl])
            ig = jax.nn.sigmoid(
                jnp.dot(xb, wgx_ref[n], preferred_element_type=F32) + bgx_ref[:, sl])
            log_a = neg_c_softplus[:, sl] * r
            a = jnp.exp(log_a)
            one_minus_a2 = -jnp.tanh(log_a) * (a * a + 1.0)
            a_ref[rows, sl] = a
            b_ref[rows, sl] = jnp.sqrt(one_minus_a2) * (ig * xcn)
            if n % 2 == 1:
                gate_piece(rc, n // 2, xn)

        for cblk in range(D_RNN // SCAN_COLS):
            if has_next:
                recurrent_piece(rc + 1, cblk, xn_next)
            cs = slice(cblk * SCAN_COLS, (cblk + 1) * SCAN_COLS)
            h_prev = hcar_ref[:, cs]
            for t in range(rc * tr // SUBLANES, (rc + 1) * tr // SUBLANES):
                trows = slice(t * SUBLANES, (t + 1) * SUBLANES)
                a = a_ref[trows, cs]
                b = b_ref[trows, cs]
                for d in (1, 2, 4):
                    keep = sub >= d
                    a_sh = jnp.where(keep, pltpu.roll(a, d, axis=0), 1.0)
                    b_sh = jnp.where(keep, pltpu.roll(b, d, axis=0), 0.0)
                    b = a * b_sh + b
                    a = a * a_sh
                h = a * h_prev + b
                b_ref[trows, cs] = h
                h_prev = jnp.broadcast_to(h[SUBLANES - 1:SUBLANES, :],
                                          (SUBLANES, SCAN_COLS))
            hcar_ref[:, cs] = h_prev

        y = jax.nn.gelu(gate_ref[rows, :], approximate=True)
        o_ref[0, rows, :] = (b_ref[rows, :] * y).astype(o_ref.dtype)
        xn = xn_next

    xbuf_ref[0:halo, :] = xbuf_ref[ts:ts + halo, :]


def _rglru(h, g, w_in, conv_w, conv_b, w_ga, b_ga, w_gx, b_gx, lam):
    batch, seq, d = h.shape
    ts = TS_REC
    row = lambda: pl.BlockSpec((1, D_RNN), lambda b, i: (0, 0))
    wblk = lambda: _resident((N_RNN_BLOCKS, RNN_BLOCK, RNN_BLOCK), lambda b, i: (0, 0, 0))
    return pl.pallas_call(
        _rglru_kernel,
        grid=(batch, seq // ts),
        in_specs=[
            pl.BlockSpec((1, ts, d), lambda b, i: (b, i, 0)),
            pl.BlockSpec((1, d), lambda b, i: (0, 0)),
            _resident((d, 2 * D_RNN), lambda b, i: (0, 0)),
            pl.BlockSpec((CONV_WIDTH, D_RNN), lambda b, i: (0, 0)),
            row(), wblk(), row(), wblk(), row(), row(),
        ],
        out_specs=pl.BlockSpec((1, ts, D_RNN), lambda b, i: (b, i, 0)),
        out_shape=jax.ShapeDtypeStruct((batch, seq, D_RNN), BF16),
        scratch_shapes=[
            pltpu.VMEM((ts + SUBLANES, D_RNN), F32),
            pltpu.VMEM((ts, D_RNN), F32),
            pltpu.VMEM((ts, D_RNN), F32),
            pltpu.VMEM((ts, D_RNN), F32),
            pltpu.VMEM((SUBLANES, D_RNN), F32),
        ],
        compiler_params=_compiler_params(2),
        name="rglru",
    )(h, g, w_in, conv_w, conv_b, w_ga, b_ga, w_gx, b_gx, lam)


def _lambda_init(layer_idx):
    return 0.8 - 0.6 * math.exp(-0.3 * layer_idx)


def kernel(x, p, g_mix, g_mlp, g_ple, w_qkv, g_q, g_k, lam_q1, lam_k1, lam_q2, lam_k2,
           g_sub, w_o_attn, w_in_rec, conv_w, conv_b, w_gate_a, b_gate_a, w_gate_x,
           b_gate_x, lam_rec, w_o_rec, w_up, w_down, w_ple_proj, w_ple_gate):
    batch, seq, d = x.shape
    depth = p.shape[0]
    t = batch * seq
    h = x.reshape(t, d)
    row = lambda v: v.reshape(1, -1)
    slopes = jnp.exp2(-8.0 * jnp.arange(1, N_HEADS + 1, dtype=F32) / N_HEADS)
    w_qkv_b = w_qkv.astype(BF16)
    w_up_b = w_up.astype(BF16)
    w_down_b = w_down.astype(BF16)
    w_ple_gate_b = w_ple_gate.astype(BF16)
    w_ple_proj_b = w_ple_proj.astype(BF16)
    p_rows = p.reshape(depth, t, -1)

    for i in range(depth):
        j = i // 2
        if i % 2 == 0:
            q, k, vt = _qkv_proj(h, row(g_mix[i]), w_qkv_b, j, row(g_q[j]), row(g_k[j]))
            o = _diff_attention(
                q.reshape(batch, seq, -1), k.reshape(batch, seq, -1), vt, slopes,
                row(lam_q1[j]), row(lam_k1[j]), row(lam_q2[j]), row(lam_k2[j]),
                g_sub[j].reshape(-1, 1), _lambda_init(i), batch, seq)
            h = _matmul_residual(o.reshape(t, -1), w_o_attn[j].astype(BF16), h)
        else:
            hy = _rglru(
                h.reshape(batch, seq, d), row(g_mix[i]), w_in_rec[j].astype(BF16), conv_w[j],
                row(conv_b[j]), w_gate_a[j].astype(BF16), row(b_gate_a[j]),
                w_gate_x[j].astype(BF16), row(b_gate_x[j]), row(lam_rec[j]))
            h = _matmul_residual(hy.reshape(t, -1), w_o_rec[j].astype(BF16), h)
        h = _mlp(h, row(g_mlp[i]), w_up_b, w_down_b, i)
        h = _ple(h, row(g_ple[i]), w_ple_gate_b, p_rows, w_ple_proj_b, i)
    return h.reshape(batch, seq, d)
```

```python
import functools
import math

import jax
import jax.numpy as jnp
from jax import lax
from jax.experimental import pallas as pl
from jax.experimental.pallas import tpu as pltpu

D_MODEL = 2048
N_HEADS = 8
HEAD_DIM_QK = 128
HEAD_DIM_V = 256
D_RNN = D_MODEL
N_RNN_BLOCKS = 8
RNN_BLOCK = D_RNN // N_RNN_BLOCKS
CONV_WIDTH = 4
RG_C = 8.0
D_FF = 4 * D_MODEL
D_PLE = 256
EPS = 1e-6
LOG2E = math.log2(math.e)

F32 = jnp.float32
BF16 = jnp.bfloat16

SUBLANES = 8
LANES = 128
VMEM_LIMIT_BYTES = 56 * 1024 * 1024

TM_PROJ = 1024
TN_PROJ = 1024
TM_RES = 512
TN_CHUNK = 512
TM_MLP = 1024
TF_MLP = 512
T_ATTN = 512
Q_BLOCKS_PER_STEP = 4
TS_REC = 512
REC_ROW_CHUNKS = 2
SCAN_COLS = 512


def _compiler_params(n_axes):
    return pltpu.CompilerParams(
        dimension_semantics=("arbitrary",) * n_axes,
        vmem_limit_bytes=VMEM_LIMIT_BYTES)


def _resident(block_shape, index_map):
    return pl.BlockSpec(block_shape, index_map, pipeline_mode=pl.Buffered(1))


def _rmsnorm_rows(x, g):
    ms = jnp.mean(x * x, axis=-1, keepdims=True)
    return x * lax.rsqrt(ms + EPS) * g


def _qkv_kernel(h_ref, g_ref, w_ref, gq_ref, gk_ref, q_ref, k_ref, vt_ref, xn_ref,
                *, n_q_tiles, n_qk_tiles):
    j = pl.program_id(1)

    @pl.when(j == 0)
    def _():
        xn_ref[...] = _rmsnorm_rows(h_ref[...], g_ref[...]).astype(BF16)

    def qk_norm_to(out_ref, gain):
        acc = jnp.dot(xn_ref[...], w_ref[...], preferred_element_type=F32)
        for c in range(acc.shape[1] // HEAD_DIM_QK):
            cs = slice(c * HEAD_DIM_QK, (c + 1) * HEAD_DIM_QK)
            blk = acc[:, cs]
            ms = jnp.mean(blk * blk, axis=-1, keepdims=True)
            out_ref[:, cs] = (blk * lax.rsqrt(ms + EPS) * gain).astype(out_ref.dtype)

    @pl.when(j < n_q_tiles)
    def _():
        qk_norm_to(q_ref, gq_ref[...] * (HEAD_DIM_QK ** -0.5 * LOG2E))

    @pl.when(jnp.logical_and(j >= n_q_tiles, j < n_qk_tiles))
    def _():
        qk_norm_to(k_ref, gk_ref[...])

    @pl.when(j >= n_qk_tiles)
    def _():
        tk = vt_ref.shape[2]
        for c in range(w_ref.shape[1] // HEAD_DIM_V):
            cs = slice(c * HEAD_DIM_V, (c + 1) * HEAD_DIM_V)
            v = jnp.dot(xn_ref[...], w_ref[:, cs], preferred_element_type=F32)
            vt = v.astype(vt_ref.dtype).T
            for kb in range(vt_ref.shape[0]):
                vt_ref[kb, cs, :] = vt[:, kb * tk:(kb + 1) * tk]


def _qkv_proj(h, g, w, layer, g_q, g_k):
    t, d = h.shape
    n = w.shape[2]
    nv = N_HEADS * HEAD_DIM_V
    nq = (n - nv) // 2
    tm, tn = TM_PROJ, TN_PROJ
    tk = T_ATTN
    n_q_tiles = nq // tn
    n_qk_tiles = 2 * n_q_tiles
    kern = functools.partial(_qkv_kernel, n_q_tiles=n_q_tiles, n_qk_tiles=n_qk_tiles)
    clamp = lambda v, lo, hi: jnp.minimum(jnp.maximum(v, lo), hi)
    return pl.pallas_call(
        kern,
        grid=(t // tm, n // tn),
        in_specs=[
            pl.BlockSpec((tm, d), lambda i, j: (i, 0)),
            pl.BlockSpec((1, d), lambda i, j: (0, 0)),
            pl.BlockSpec((None, d, tn), lambda i, j: (layer, 0, j)),
            pl.BlockSpec((1, HEAD_DIM_QK), lambda i, j: (0, 0)),
            pl.BlockSpec((1, HEAD_DIM_QK), lambda i, j: (0, 0)),
        ],
        out_specs=[
            pl.BlockSpec((tm, tn), lambda i, j: (i, jnp.minimum(j, n_q_tiles - 1))),
            pl.BlockSpec((tm, tn), lambda i, j: (i, clamp(j - n_q_tiles, 0, n_q_tiles - 1))),
            pl.BlockSpec((tm // tk, tn, tk),
                         lambda i, j: (i, clamp(j - n_qk_tiles, 0, nv // tn - 1), 0)),
        ],
        out_shape=[
            jax.ShapeDtypeStruct((t, nq), BF16),
            jax.ShapeDtypeStruct((t, nq), BF16),
            jax.ShapeDtypeStruct((t // tk, nv, tk), BF16),
        ],
        scratch_shapes=[pltpu.VMEM((tm, d), BF16)],
        compiler_params=_compiler_params(2),
        name="qkv_proj",
    )(h, g, w, g_q, g_k)


def _matmul_residual_kernel(a_ref, w_ref, h_ref, o_ref):
    a = a_ref[...]
    for c in range(o_ref.shape[1] // TN_CHUNK):
        cs = slice(c * TN_CHUNK, (c + 1) * TN_CHUNK)
        o_ref[:, cs] = h_ref[:, cs] + jnp.dot(a, w_ref[:, cs], preferred_element_type=F32)


def _matmul_residual(a, w, h):
    t, k = a.shape
    n = w.shape[1]
    tm = TM_RES
    return pl.pallas_call(
        _matmul_residual_kernel,
        grid=(t // tm,),
        in_specs=[
            pl.BlockSpec((tm, k), lambda i: (i, 0)),
            _resident((k, n), lambda i: (0, 0)),
            pl.BlockSpec((tm, n), lambda i: (i, 0)),
        ],
        out_specs=pl.BlockSpec((tm, n), lambda i: (i, 0)),
        out_shape=jax.ShapeDtypeStruct((t, n), F32),
        compiler_params=_compiler_params(1),
        name="matmul_residual",
    )(a, w, h)


def _mlp_kernel(h_ref, g_ref, wu_ref, wd_ref, o_ref, xn_ref):
    @pl.when(pl.program_id(1) == 0)
    def _():
        x = h_ref[...]
        xn_ref[...] = _rmsnorm_rows(x, g_ref[...]).astype(BF16)
        o_ref[...] = x

    u = jnp.dot(xn_ref[...], wu_ref[...], preferred_element_type=F32)
    a = jnp.square(jnp.maximum(u, 0.0)).astype(BF16)
    o_ref[...] += jnp.dot(a, wd_ref[...], preferred_element_type=F32)


def _mlp(h, g, w_up, w_down, layer):
    t, d = h.shape
    f = w_up.shape[2]
    tm, tf = TM_MLP, TF_MLP
    return pl.pallas_call(
        _mlp_kernel,
        grid=(t // tm, f // tf),
        in_specs=[
            pl.BlockSpec((tm, d), lambda i, j: (i, 0)),
            pl.BlockSpec((1, d), lambda i, j: (0, 0)),
            pl.BlockSpec((None, d, tf), lambda i, j: (layer, 0, j)),
            pl.BlockSpec((None, tf, d), lambda i, j: (layer, j, 0)),
        ],
        out_specs=pl.BlockSpec((tm, d), lambda i, j: (i, 0)),
        out_shape=jax.ShapeDtypeStruct((t, d), F32),
        scratch_shapes=[pltpu.VMEM((tm, d), BF16)],
        compiler_params=_compiler_params(2),
        name="mlp",
    )(h, g, w_up, w_down)


def _ple_kernel(h_ref, g_ref, wg_ref, p_ref, wp_ref, o_ref):
    xn = _rmsnorm_rows(h_ref[...], g_ref[...]).astype(BF16)
    pb = p_ref[...].astype(BF16)
    for c in range(o_ref.shape[1] // TN_CHUNK):
        cs = slice(c * TN_CHUNK, (c + 1) * TN_CHUNK)
        gate = jax.nn.sigmoid(jnp.dot(xn, wg_ref[:, cs], preferred_element_type=F32))
        proj = jnp.dot(pb, wp_ref[:, cs], preferred_element_type=F32)
        o_ref[:, cs] = h_ref[:, cs] + gate * proj


def _ple(h, g, w_gate, p, w_proj, layer):
    t, d = h.shape
    dp = p.shape[2]
    tm = TM_RES
    return pl.pallas_call(
        _ple_kernel,
        grid=(t // tm,),
        in_specs=[
            pl.BlockSpec((tm, d), lambda i: (i, 0)),
            pl.BlockSpec((1, d), lambda i: (0, 0)),
            _resident((None, d, d), lambda i: (layer, 0, 0)),
            pl.BlockSpec((None, tm, dp), lambda i: (layer, i, 0)),
            _resident((None, dp, d), lambda i: (layer, 0, 0)),
        ],
        out_specs=pl.BlockSpec((tm, d), lambda i: (i, 0)),
        out_shape=jax.ShapeDtypeStruct((t, d), F32),
        compiler_params=_compiler_params(1),
        name="ple",
    )(h, g, w_gate, p, w_proj)


def _attn_kernel(slopes_ref, q_ref, k_ref, vt_ref, lq1_ref, lk1_ref, lq2_ref, lk2_ref,
                 gsub_ref, o_ref, bias_ref, dbias_ref, st0_ref, st1_ref, m_ref, l_ref, acc_ref,
                 *, lam0):
    tq = q_ref.shape[1] // Q_BLOCKS_PER_STEP
    tk = vt_ref.shape[2]
    head = pl.program_id(1)
    step = pl.program_id(2)
    slope = slopes_ref[head] * LOG2E

    @pl.when(step == 0)
    def _():
        row = lax.broadcasted_iota(jnp.int32, (tk, tq), 0)
        col = lax.broadcasted_iota(jnp.int32, (tk, tq), 1)
        bias = slope * row.astype(F32)
        bias_ref[...] = bias
        dbias_ref[...] = jnp.where(row <= col, bias, -jnp.inf)

    lam = (jnp.exp(jnp.sum(lq1_ref[...] * lk1_ref[...], axis=-1, keepdims=True))
           - jnp.exp(jnp.sum(lq2_ref[...] * lk2_ref[...], axis=-1, keepdims=True))
           + lam0)
    st_refs = (st0_ref, st1_ref)

    def query_block(i, qrows, odd):
        m_ref[...] = jnp.full(m_ref.shape, -jnp.inf, F32)
        l_ref[...] = jnp.zeros(l_ref.shape, F32)
        acc_ref[...] = jnp.zeros(acc_ref.shape, F32)
        q = q_ref[0, qrows, :]

        def scores(j, c):
            start = pl.multiple_of(j * tk, tk)
            cs = slice(c * HEAD_DIM_QK, (c + 1) * HEAD_DIM_QK)
            st_refs[c][...] = lax.dot_general(
                k_ref[0, pl.ds(start, tk), cs], q[:, cs], (((1,), (1,)), ((), ())),
                preferred_element_type=F32)

        def softmax_pv(j, c, b_ref):
            off = slope * ((j - i) * tk).astype(F32)
            st = st_refs[c][...] + b_ref[...]
            m_prev = m_ref[c]
            m_new = jnp.maximum(m_prev, jnp.max(st, axis=0, keepdims=True) + off)
            alpha = jnp.exp2(m_prev - m_new)
            p = jnp.exp2(st - (m_new - off))
            l_ref[c] = alpha * l_ref[c] + jnp.sum(p, axis=0, keepdims=True)
            acc_ref[c] = alpha * acc_ref[c] + jnp.dot(
                vt_ref[j], p.astype(BF16), preferred_element_type=F32)
            m_ref[c] = m_new

        scores(0, 0)

        def full_block(j):
            scores(j, 1)
            softmax_pv(j, 0, bias_ref)
            scores(j + 1, 0)
            softmax_pv(j, 1, bias_ref)

        def block_pair(jj, carry):
            full_block(2 * jj)
            full_block(2 * jj + 1)
            return carry

        lax.fori_loop(0, lax.shift_right_logical(i, 1), block_pair, 0)

        if odd:
            full_block(i - 1)

        scores(i, 1)
        softmax_pv(i, 0, dbias_ref)
        softmax_pv(i, 1, dbias_ref)

        ot = (acc_ref[0] * (1.0 / l_ref[0])
              - acc_ref[1] * (lam * (1.0 / l_ref[1])))
        ms = jnp.mean(ot * ot, axis=0, keepdims=True)
        ot = ot * lax.rsqrt(ms + EPS) * (gsub_ref[...] * (1.0 - lam0))
        o_ref[0, qrows, :] = ot.astype(o_ref.dtype).T

    assert Q_BLOCKS_PER_STEP % 2 == 0
    for t in range(Q_BLOCKS_PER_STEP):
        query_block(step * Q_BLOCKS_PER_STEP + t, slice(t * tq, (t + 1) * tq), t % 2 == 1)


def _diff_attention(q, k, vt, slopes, lam_q1, lam_k1, lam_q2, lam_k2, g_sub, lam0, batch, seq):
    tq = tk = T_ATTN
    tqs = tq * Q_BLOCKS_PER_STEP
    nh = N_HEADS
    blk = 2 * HEAD_DIM_QK
    nkv = seq // tk
    vec = lambda: pl.BlockSpec((1, HEAD_DIM_QK), lambda b, h, i: (0, 0))
    kern = functools.partial(_attn_kernel, lam0=lam0)
    return pl.pallas_call(
        kern,
        grid=(batch, nh, seq // tqs),
        in_specs=[
            pl.BlockSpec(memory_space=pltpu.SMEM),
            pl.BlockSpec((1, tqs, blk), lambda b, h, i: (b, i, h)),
            pl.BlockSpec((1, seq, blk), lambda b, h, i: (b, 0, h)),
            pl.BlockSpec((nkv, HEAD_DIM_V, tk), lambda b, h, i: (b, h, 0)),
            vec(), vec(), vec(), vec(),
            pl.BlockSpec((HEAD_DIM_V, 1), lambda b, h, i: (0, 0)),
        ],
        out_specs=pl.BlockSpec((1, tqs, HEAD_DIM_V), lambda b, h, i: (b, i, h)),
        out_shape=jax.ShapeDtypeStruct((batch, seq, nh * HEAD_DIM_V), BF16),
        scratch_shapes=[
            pltpu.VMEM((tk, tq), F32),
            pltpu.VMEM((tk, tq), F32),
            pltpu.VMEM((tk, tq), F32),
            pltpu.VMEM((tk, tq), F32),
            pltpu.VMEM((2, 1, tq), F32),
            pltpu.VMEM((2, 1, tq), F32),
            pltpu.VMEM((2, HEAD_DIM_V, tq), F32),
        ],
        compiler_params=_compiler_params(3),
        name="diff_attention",
    )(slopes, q, k, vt, lam_q1, lam_k1, lam_q2, lam_k2, g_sub)


def _rglru_kernel(h_ref, g_ref, win_ref, cw_ref, cb_ref, wga_ref, bga_ref, wgx_ref, bgx_ref,
                  lam_ref, o_ref, xbuf_ref, gate_ref, a_ref, b_ref, hcar_ref):
    ts = h_ref.shape[1]
    halo = SUBLANES

    @pl.when(pl.program_id(1) == 0)
    def _():
        xbuf_ref[0:halo, :] = jnp.zeros((halo, D_RNN), F32)
        hcar_ref[...] = jnp.zeros(hcar_ref.shape, F32)

    cw = cw_ref[...]
    neg_c_softplus = -RG_C * jax.nn.softplus(-lam_ref[...])
    sub = lax.broadcasted_iota(jnp.int32, (SUBLANES, SCAN_COLS), 0)

    tr = ts // REC_ROW_CHUNKS
    n_pieces = D_RNN // TN_CHUNK
    assert N_RNN_BLOCKS == 2 * n_pieces and D_RNN // SCAN_COLS == n_pieces

    def norm_rows(rc):
        return _rmsnorm_rows(h_ref[0, rc * tr:(rc + 1) * tr, :], g_ref[...]).astype(BF16)

    def recurrent_piece(rc, c, xn):
        xbuf_ref[halo + rc * tr:halo + (rc + 1) * tr, c * TN_CHUNK:(c + 1) * TN_CHUNK] = jnp.dot(
            xn, win_ref[:, D_RNN + c * TN_CHUNK:D_RNN + (c + 1) * TN_CHUNK],
            preferred_element_type=F32)

    def gate_piece(rc, c, xn):
        cs = slice(c * TN_CHUNK, (c + 1) * TN_CHUNK)
        gate_ref[rc * tr:(rc + 1) * tr, cs] = jnp.dot(
            xn, win_ref[:, cs], preferred_element_type=F32)

    xn = norm_rows(0)
    for c in range(n_pieces):
        recurrent_piece(0, c, xn)

    for rc in range(REC_ROW_CHUNKS):
        rows = slice(rc * tr, (rc + 1) * tr)
        xrows = slice(halo + rc * tr, halo + (rc + 1) * tr)
        has_next = rc + 1 < REC_ROW_CHUNKS

        xc = cb_ref[...] + cw[CONV_WIDTH - 1:CONV_WIDTH, :] * xbuf_ref[xrows, :]
        for back in range(1, CONV_WIDTH):
            w_row = cw[CONV_WIDTH - 1 - back:CONV_WIDTH - back, :]
            xc = xc + w_row * xbuf_ref[halo + rc * tr - back:halo + (rc + 1) * tr - back, :]
        xn_next = norm_rows(rc + 1) if has_next else None

        for n in range(N_RNN_BLOCKS):
            sl = slice(n * RNN_BLOCK, (n + 1) * RNN_BLOCK)
            xcn = xc[:, sl]
            xb = xcn.astype(BF16)
            r = jax.nn.sigmoid(
                jnp.dot(xb, wga_ref[n], preferred_element_type=F32) + bga_ref[:, sl])
            ig = jax.nn.sigmoid(
                jnp.dot(xb, wgx_ref[n], preferred_element_type=F32) + bgx_ref[:, sl])
            log_a = neg_c_softplus[:, sl] * r
            a = jnp.exp(log_a)
            one_minus_a2 = -jnp.tanh(log_a) * (a * a + 1.0)
            a_ref[rows, sl] = a
            b_ref[rows, sl] = jnp.sqrt(one_minus_a2) * (ig * xcn)
            if n % 2 == 1:
                gate_piece(rc, n // 2, xn)

        for cblk in range(D_RNN // SCAN_COLS):
            if has_next:
                recurrent_piece(rc + 1, cblk, xn_next)
            cs = slice(cblk * SCAN_COLS, (cblk + 1) * SCAN_COLS)
            h_prev = hcar_ref[:, cs]
            for t in range(rc * tr // SUBLANES, (rc + 1) * tr // SUBLANES):
                trows = slice(t * SUBLANES, (t + 1) * SUBLANES)
                a = a_ref[trows, cs]
                b = b_ref[trows, cs]
                for d in (1, 2, 4):
                    keep = sub >= d
                    a_sh = jnp.where(keep, pltpu.roll(a, d, axis=0), 1.0)
                    b_sh = jnp.where(keep, pltpu.roll(b, d, axis=0), 0.0)
                    b = a * b_sh + b
                    a = a * a_sh
                h = a * h_prev + b
                b_ref[trows, cs] = h
                h_prev = jnp.broadcast_to(h[SUBLANES - 1:SUBLANES, :],
                                          (SUBLANES, SCAN_COLS))
            hcar_ref[:, cs] = h_prev

        y = jax.nn.gelu(gate_ref[rows, :], approximate=True)
        o_ref[0, rows, :] = (b_ref[rows, :] * y).astype(o_ref.dtype)
        xn = xn_next

    xbuf_ref[0:halo, :] = xbuf_ref[ts:ts + halo, :]


def _rglru(h, g, w_in, conv_w, conv_b, w_ga, b_ga, w_gx, b_gx, lam):
    batch, seq, d = h.shape
    ts = TS_REC
    row = lambda: pl.BlockSpec((1, D_RNN), lambda b, i: (0, 0))
    wblk = lambda: _resident((N_RNN_BLOCKS, RNN_BLOCK, RNN_BLOCK), lambda b, i: (0, 0, 0))
    return pl.pallas_call(
        _rglru_kernel,
        grid=(batch, seq // ts),
        in_specs=[
            pl.BlockSpec((1, ts, d), lambda b, i: (b, i, 0)),
            pl.BlockSpec((1, d), lambda b, i: (0, 0)),
            _resident((d, 2 * D_RNN), lambda b, i: (0, 0)),
            pl.BlockSpec((CONV_WIDTH, D_RNN), lambda b, i: (0, 0)),
            row(), wblk(), row(), wblk(), row(), row(),
        ],
        out_specs=pl.BlockSpec((1, ts, D_RNN), lambda b, i: (b, i, 0)),
        out_shape=jax.ShapeDtypeStruct((batch, seq, D_RNN), BF16),
        scratch_shapes=[
            pltpu.VMEM((ts + SUBLANES, D_RNN), F32),
            pltpu.VMEM((ts, D_RNN), F32),
            pltpu.VMEM((ts, D_RNN), F32),
            pltpu.VMEM((ts, D_RNN), F32),
            pltpu.VMEM((SUBLANES, D_RNN), F32),
        ],
        compiler_params=_compiler_params(2),
        name="rglru",
    )(h, g, w_in, conv_w, conv_b, w_ga, b_ga, w_gx, b_gx, lam)


def _lambda_init(layer_idx):
    return 0.8 - 0.6 * math.exp(-0.3 * layer_idx)


def kernel(x, p, g_mix, g_mlp, g_ple, w_qkv, g_q, g_k, lam_q1, lam_k1, lam_q2, lam_k2,
           g_sub, w_o_attn, w_in_rec, conv_w, conv_b, w_gate_a, b_gate_a, w_gate_x,
           b_gate_x, lam_rec, w_o_rec, w_up, w_down, w_ple_proj, w_ple_gate):
    batch, seq, d = x.shape
    depth = p.shape[0]
    t = batch * seq
    h = x.reshape(t, d)
    row = lambda v: v.reshape(1, -1)
    slopes = jnp.exp2(-8.0 * jnp.arange(1, N_HEADS + 1, dtype=F32) / N_HEADS)
    w_qkv_b = w_qkv.astype(BF16)
    w_up_b = w_up.astype(BF16)
    w_down_b = w_down.astype(BF16)
    w_ple_gate_b = w_ple_gate.astype(BF16)
    w_ple_proj_b = w_ple_proj.astype(BF16)
    p_rows = p.reshape(depth, t, -1)

    for i in range(depth):
        j = i // 2
        if i % 2 == 0:
            q, k, vt = _qkv_proj(h, row(g_mix[i]), w_qkv_b, j, row(g_q[j]), row(g_k[j]))
            o = _diff_attention(
                q.reshape(batch, seq, -1), k.reshape(batch, seq, -1), vt, slopes,
                row(lam_q1[j]), row(lam_k1[j]), row(lam_q2[j]), row(lam_k2[j]),
                g_sub[j].reshape(-1, 1), _lambda_init(i), batch, seq)
            h = _matmul_residual(o.reshape(t, -1), w_o_attn[j].astype(BF16), h)
        else:
            hy = _rglru(
                h.reshape(batch, seq, d), row(g_mix[i]), w_in_rec[j].astype(BF16), conv_w[j],
                row(conv_b[j]), w_gate_a[j].astype(BF16), row(b_gate_a[j]),
                w_gate_x[j].astype(BF16), row(b_gate_x[j]), row(lam_rec[j]))
            h = _matmul_residual(hy.reshape(t, -1), w_o_rec[j].astype(BF16), h)
        h = _mlp(h, row(g_mlp[i]), w_up_b, w_down_b, i)
        h = _ple(h, row(g_ple[i]), w_ple_gate_b, p_rows, w_ple_proj_b, i)
    return h.reshape(batch, seq, d)
```

```python
import functools
import math

import jax
import jax.numpy as jnp
from jax import lax
from jax.experimental import pallas as pl
from jax.experimental.pallas import tpu as pltpu

D_MODEL = 2048
N_HEADS = 8
HEAD_DIM_QK = 128
HEAD_DIM_V = 256
D_RNN = D_MODEL
N_RNN_BLOCKS = 8
RNN_BLOCK = D_RNN // N_RNN_BLOCKS
CONV_WIDTH = 4
RG_C = 8.0
D_FF = 4 * D_MODEL
D_PLE = 256
EPS = 1e-6
LOG2E = math.log2(math.e)

F32 = jnp.float32
BF16 = jnp.bfloat16

SUBLANES = 8
LANES = 128
VMEM_LIMIT_BYTES = 56 * 1024 * 1024

TM_PROJ = 1024
TN_PROJ = 1024
QKV_NORM_CHUNKS = 4
TM_RES = 512
TN_CHUNK = 512
PLE_ROW_CHUNKS = 2
TM_MLP = 1024
TF_MLP = 512
MLP_NORM_CHUNKS = 4
T_ATTN = 512
Q_BLOCKS_PER_STEP = 4
TS_REC = 512
REC_ROW_CHUNKS = 2
SCAN_COLS = 512


def _compiler_params(n_axes):
    return pltpu.CompilerParams(
        dimension_semantics=("arbitrary",) * n_axes,
        vmem_limit_bytes=VMEM_LIMIT_BYTES)


def _resident(block_shape, index_map):
    return pl.BlockSpec(block_shape, index_map, pipeline_mode=pl.Buffered(1))


def _rmsnorm_rows(x, g):
    ms = jnp.mean(x * x, axis=-1, keepdims=True)
    return x * lax.rsqrt(ms + EPS) * g


def _qkv_kernel(h_ref, g_ref, w_ref, gq_ref, gk_ref, q_ref, k_ref, vt_ref, xn_ref,
                *, n_q_tiles, n_qk_tiles):
    j = pl.program_id(1)

    def qk_norm_to(out_ref, gain, xn, rows):
        acc = jnp.dot(xn, w_ref[...], preferred_element_type=F32)
        for c in range(acc.shape[1] // HEAD_DIM_QK):
            cs = slice(c * HEAD_DIM_QK, (c + 1) * HEAD_DIM_QK)
            blk = acc[:, cs]
            ms = jnp.mean(blk * blk, axis=-1, keepdims=True)
            out_ref[rows, cs] = (blk * lax.rsqrt(ms + EPS) * gain).astype(out_ref.dtype)

    all_rows = slice(0, h_ref.shape[0])
    q_gain = gq_ref[...] * (HEAD_DIM_QK ** -0.5 * LOG2E)

    @pl.when(j == 0)
    def _():
        tr = h_ref.shape[0] // QKV_NORM_CHUNKS
        for r in range(QKV_NORM_CHUNKS):
            rows = slice(r * tr, (r + 1) * tr)
            xn = _rmsnorm_rows(h_ref[rows, :], g_ref[...]).astype(BF16)
            xn_ref[rows, :] = xn
            qk_norm_to(q_ref, q_gain, xn, rows)

    @pl.when(jnp.logical_and(j > 0, j < n_q_tiles))
    def _():
        qk_norm_to(q_ref, q_gain, xn_ref[...], all_rows)

    @pl.when(jnp.logical_and(j >= n_q_tiles, j < n_qk_tiles))
    def _():
        qk_norm_to(k_ref, gk_ref[...], xn_ref[...], all_rows)

    @pl.when(j >= n_qk_tiles)
    def _():
        tk = vt_ref.shape[2]
        for c in range(w_ref.shape[1] // HEAD_DIM_V):
            cs = slice(c * HEAD_DIM_V, (c + 1) * HEAD_DIM_V)
            v = jnp.dot(xn_ref[...], w_ref[:, cs], preferred_element_type=F32)
            vt = v.astype(vt_ref.dtype).T
            for kb in range(vt_ref.shape[0]):
                vt_ref[kb, cs, :] = vt[:, kb * tk:(kb + 1) * tk]


def _qkv_proj(h, g, w, layer, g_q, g_k):
    t, d = h.shape
    n = w.shape[2]
    nv = N_HEADS * HEAD_DIM_V
    nq = (n - nv) // 2
    tm, tn = TM_PROJ, TN_PROJ
    tk = T_ATTN
    n_q_tiles = nq // tn
    n_qk_tiles = 2 * n_q_tiles
    kern = functools.partial(_qkv_kernel, n_q_tiles=n_q_tiles, n_qk_tiles=n_qk_tiles)
    clamp = lambda v, lo, hi: jnp.minimum(jnp.maximum(v, lo), hi)
    return pl.pallas_call(
        kern,
        grid=(t // tm, n // tn),
        in_specs=[
            pl.BlockSpec((tm, d), lambda i, j: (i, 0)),
            pl.BlockSpec((1, d), lambda i, j: (0, 0)),
            pl.BlockSpec((None, d, tn), lambda i, j: (layer, 0, j)),
            pl.BlockSpec((1, HEAD_DIM_QK), lambda i, j: (0, 0)),
            pl.BlockSpec((1, HEAD_DIM_QK), lambda i, j: (0, 0)),
        ],
        out_specs=[
            pl.BlockSpec((tm, tn), lambda i, j: (i, jnp.minimum(j, n_q_tiles - 1))),
            pl.BlockSpec((tm, tn), lambda i, j: (i, clamp(j - n_q_tiles, 0, n_q_tiles - 1))),
            pl.BlockSpec((tm // tk, tn, tk),
                         lambda i, j: (i, clamp(j - n_qk_tiles, 0, nv // tn - 1), 0)),
        ],
        out_shape=[
            jax.ShapeDtypeStruct((t, nq), BF16),
            jax.ShapeDtypeStruct((t, nq), BF16),
            jax.ShapeDtypeStruct((t // tk, nv, tk), BF16),
        ],
        scratch_shapes=[pltpu.VMEM((tm, d), BF16)],
        compiler_params=_compiler_params(2),
        name="qkv_proj",
    )(h, g, w, g_q, g_k)


def _matmul_residual_kernel(a_ref, w_ref, h_ref, o_ref):
    a = a_ref[...]
    for c in range(o_ref.shape[1] // TN_CHUNK):
        cs = slice(c * TN_CHUNK, (c + 1) * TN_CHUNK)
        o_ref[:, cs] = h_ref[:, cs] + jnp.dot(a, w_ref[:, cs], preferred_element_type=F32)


def _matmul_residual(a, w, h):
    t, k = a.shape
    n = w.shape[1]
    tm = TM_RES
    return pl.pallas_call(
        _matmul_residual_kernel,
        grid=(t // tm,),
        in_specs=[
            pl.BlockSpec((tm, k), lambda i: (i, 0)),
            _resident((k, n), lambda i: (0, 0)),
            pl.BlockSpec((tm, n), lambda i: (i, 0)),
        ],
        out_specs=pl.BlockSpec((tm, n), lambda i: (i, 0)),
        out_shape=jax.ShapeDtypeStruct((t, n), F32),
        compiler_params=_compiler_params(1),
        name="matmul_residual",
    )(a, w, h)


def _mlp_kernel(h_ref, g_ref, wu_ref, wd_ref, o_ref, xn_ref):
    def ff_tile(xn):
        u = jnp.dot(xn, wu_ref[...], preferred_element_type=F32)
        a = jnp.square(jnp.maximum(u, 0.0)).astype(BF16)
        return jnp.dot(a, wd_ref[...], preferred_element_type=F32)

    @pl.when(pl.program_id(1) == 0)
    def _():
        tr = o_ref.shape[0] // MLP_NORM_CHUNKS
        for r in range(MLP_NORM_CHUNKS):
            rows = slice(r * tr, (r + 1) * tr)
            x = h_ref[rows, :]
            xn = _rmsnorm_rows(x, g_ref[...]).astype(BF16)
            xn_ref[rows, :] = xn
            o_ref[rows, :] = x + ff_tile(xn)

    @pl.when(pl.program_id(1) != 0)
    def _():
        o_ref[...] += ff_tile(xn_ref[...])


def _mlp(h, g, w_up, w_down, layer):
    t, d = h.shape
    f = w_up.shape[2]
    tm, tf = TM_MLP, TF_MLP
    return pl.pallas_call(
        _mlp_kernel,
        grid=(t // tm, f // tf),
        in_specs=[
            pl.BlockSpec((tm, d), lambda i, j: (i, 0)),
            pl.BlockSpec((1, d), lambda i, j: (0, 0)),
            pl.BlockSpec((None, d, tf), lambda i, j: (layer, 0, j)),
            pl.BlockSpec((None, tf, d), lambda i, j: (layer, j, 0)),
        ],
        out_specs=pl.BlockSpec((tm, d), lambda i, j: (i, 0)),
        out_shape=jax.ShapeDtypeStruct((t, d), F32),
        scratch_shapes=[pltpu.VMEM((tm, d), BF16)],
        compiler_params=_compiler_params(2),
        name="mlp",
    )(h, g, w_up, w_down)


def _ple_kernel(h_ref, g_ref, wg_ref, p_ref, wp_ref, o_ref):
    tr = o_ref.shape[0] // PLE_ROW_CHUNKS
    for r in range(PLE_ROW_CHUNKS):
        rows = slice(r * tr, (r + 1) * tr)
        xn = _rmsnorm_rows(h_ref[rows, :], g_ref[...]).astype(BF16)
        pb = p_ref[rows, :].astype(BF16)
        for c in range(o_ref.shape[1] // TN_CHUNK):
            cs = slice(c * TN_CHUNK, (c + 1) * TN_CHUNK)
            gate = jax.nn.sigmoid(jnp.dot(xn, wg_ref[:, cs], preferred_element_type=F32))
            proj = jnp.dot(pb, wp_ref[:, cs], preferred_element_type=F32)
            o_ref[rows, cs] = h_ref[rows, cs] + gate * proj


def _ple(h, g, w_gate, p, w_proj, layer):
    t, d = h.shape
    dp = p.shape[2]
    tm = TM_RES
    return pl.pallas_call(
        _ple_kernel,
        grid=(t // tm,),
        in_specs=[
            pl.BlockSpec((tm, d), lambda i: (i, 0)),
            pl.BlockSpec((1, d), lambda i: (0, 0)),
            _resident((None, d, d), lambda i: (layer, 0, 0)),
            pl.BlockSpec((None, tm, dp), lambda i: (layer, i, 0)),
            _resident((None, dp, d), lambda i: (layer, 0, 0)),
        ],
        out_specs=pl.BlockSpec((tm, d), lambda i: (i, 0)),
        out_shape=jax.ShapeDtypeStruct((t, d), F32),
        compiler_params=_compiler_params(1),
        name="ple",
    )(h, g, w_gate, p, w_proj)


def _attn_kernel(slopes_ref, q_ref, k_ref, vt_ref, lq1_ref, lk1_ref, lq2_ref, lk2_ref,
                 gsub_ref, o_ref, bias_ref, dbias_ref, st0_ref, st1_ref, m_ref, l_ref, acc_ref,
                 *, lam0):
    tq = q_ref.shape[1] // Q_BLOCKS_PER_STEP
    tk = vt_ref.shape[2]
    head = pl.program_id(1)
    step = pl.program_id(2)
    slope = slopes_ref[head] * LOG2E

    @pl.when(step == 0)
    def _():
        row = lax.broadcasted_iota(jnp.int32, (tk, tq), 0)
        col = lax.broadcasted_iota(jnp.int32, (tk, tq), 1)
        bias = slope * row.astype(F32)
        bias_ref[...] = bias
        dbias_ref[...] = jnp.where(row <= col, bias, -jnp.inf)

    lam = (jnp.exp(jnp.sum(lq1_ref[...] * lk1_ref[...], axis=-1, keepdims=True))
           - jnp.exp(jnp.sum(lq2_ref[...] * lk2_ref[...], axis=-1, keepdims=True))
           + lam0)
    st_refs = (st0_ref, st1_ref)

    def query_block(i, qrows, odd):
        m_ref[...] = jnp.full(m_ref.shape, -jnp.inf, F32)
        l_ref[...] = jnp.zeros(l_ref.shape, F32)
        acc_ref[...] = jnp.zeros(acc_ref.shape, F32)
        q = q_ref[0, qrows, :]

        def scores(j, c):
            start = pl.multiple_of(j * tk, tk)
            cs = slice(c * HEAD_DIM_QK, (c + 1) * HEAD_DIM_QK)
            st_refs[c][...] = lax.dot_general(
                k_ref[0, pl.ds(start, tk), cs], q[:, cs], (((1,), (1,)), ((), ())),
                preferred_element_type=F32)

        def softmax_pv(j, c, b_ref):
            off = slope * ((j - i) * tk).astype(F32)
            st = st_refs[c][...] + b_ref[...]
            m_prev = m_ref[c]
            m_new = jnp.maximum(m_prev, jnp.max(st, axis=0, keepdims=True) + off)
            alpha = jnp.exp2(m_prev - m_new)
            p = jnp.exp2(st - (m_new - off))
            l_ref[c] = alpha * l_ref[c] + jnp.sum(p, axis=0, keepdims=True)
            acc_ref[c] = alpha * acc_ref[c] + jnp.dot(
                vt_ref[j], p.astype(BF16), preferred_element_type=F32)
            m_ref[c] = m_new

        scores(0, 0)

        def full_block(j):
            scores(j, 1)
            softmax_pv(j, 0, bias_ref)
            scores(j + 1, 0)
            softmax_pv(j, 1, bias_ref)

        def block_pair(jj, carry):
            full_block(2 * jj)
            full_block(2 * jj + 1)
            return carry

        lax.fori_loop(0, lax.shift_right_logical(i, 1), block_pair, 0)

        if odd:
            full_block(i - 1)

        scores(i, 1)
        softmax_pv(i, 0, dbias_ref)
        softmax_pv(i, 1, dbias_ref)

        ot = (acc_ref[0] * (1.0 / l_ref[0])
              - acc_ref[1] * (lam * (1.0 / l_ref[1])))
        ms = jnp.mean(ot * ot, axis=0, keepdims=True)
        ot = ot * lax.rsqrt(ms + EPS) * (gsub_ref[...] * (1.0 - lam0))
        o_ref[0, qrows, :] = ot.astype(o_ref.dtype).T

    assert Q_BLOCKS_PER_STEP % 2 == 0
    for t in range(Q_BLOCKS_PER_STEP):
        query_block(step * Q_BLOCKS_PER_STEP + t, slice(t * tq, (t + 1) * tq), t % 2 == 1)


def _diff_attention(q, k, vt, slopes, lam_q1, lam_k1, lam_q2, lam_k2, g_sub, lam0, batch, seq):
    tq = tk = T_ATTN
    tqs = tq * Q_BLOCKS_PER_STEP
    nh = N_HEADS
    blk = 2 * HEAD_DIM_QK
    nkv = seq // tk
    vec = lambda: pl.BlockSpec((1, HEAD_DIM_QK), lambda b, h, i: (0, 0))
    kern = functools.partial(_attn_kernel, lam0=lam0)
    return pl.pallas_call(
        kern,
        grid=(batch, nh, seq // tqs),
        in_specs=[
            pl.BlockSpec(memory_space=pltpu.SMEM),
            pl.BlockSpec((1, tqs, blk), lambda b, h, i: (b, i, h)),
            pl.BlockSpec((1, seq, blk), lambda b, h, i: (b, 0, h)),
            pl.BlockSpec((nkv, HEAD_DIM_V, tk), lambda b, h, i: (b, h, 0)),
            vec(), vec(), vec(), vec(),
            pl.BlockSpec((HEAD_DIM_V, 1), lambda b, h, i: (0, 0)),
        ],
        out_specs=pl.BlockSpec((1, tqs, HEAD_DIM_V), lambda b, h, i: (b, i, h)),
        out_shape=jax.ShapeDtypeStruct((batch, seq, nh * HEAD_DIM_V), BF16),
        scratch_shapes=[
            pltpu.VMEM((tk, tq), F32),
            pltpu.VMEM((tk, tq), F32),
            pltpu.VMEM((tk, tq), F32),
            pltpu.VMEM((tk, tq), F32),
            pltpu.VMEM((2, 1, tq), F32),
            pltpu.VMEM((2, 1, tq), F32),
            pltpu.VMEM((2, HEAD_DIM_V, tq), F32),
        ],
        compiler_params=_compiler_params(3),
        name="diff_attention",
    )(slopes, q, k, vt, lam_q1, lam_k1, lam_q2, lam_k2, g_sub)


def _rglru_kernel(h_ref, g_ref, win_ref, cw_ref, cb_ref, wga_ref, bga_ref, wgx_ref, bgx_ref,
                  lam_ref, o_ref, xbuf_ref, gate_ref, a_ref, b_ref, hcar_ref):
    ts = h_ref.shape[1]
    halo = SUBLANES

    @pl.when(pl.program_id(1) == 0)
    def _():
        xbuf_ref[0:halo, :] = jnp.zeros((halo, D_RNN), F32)
        hcar_ref[...] = jnp.zeros(hcar_ref.shape, F32)

    cw = cw_ref[...]
    neg_c_softplus = -RG_C * jax.nn.softplus(-lam_ref[...])
    sub = lax.broadcasted_iota(jnp.int32, (SUBLANES, SCAN_COLS), 0)

    tr = ts // REC_ROW_CHUNKS
    n_pieces = D_RNN // TN_CHUNK
    assert N_RNN_BLOCKS == 2 * n_pieces and D_RNN // SCAN_COLS == n_pieces

    def norm_rows(rc):
        return _rmsnorm_rows(h_ref[0, rc * tr:(rc + 1) * tr, :], g_ref[...]).astype(BF16)

    def recurrent_piece(rc, c, xn):
        xbuf_ref[halo + rc * tr:halo + (rc + 1) * tr, c * TN_CHUNK:(c + 1) * TN_CHUNK] = jnp.dot(
            xn, win_ref[:, D_RNN + c * TN_CHUNK:D_RNN + (c + 1) * TN_CHUNK],
            preferred_element_type=F32)

    def gate_piece(rc, c, xn):
        cs = slice(c * TN_CHUNK, (c + 1) * TN_CHUNK)
        gate_ref[rc * tr:(rc + 1) * tr, cs] = jnp.dot(
            xn, win_ref[:, cs], preferred_element_type=F32)

    xn = norm_rows(0)
    for c in range(n_pieces):
        recurrent_piece(0, c, xn)

    for rc in range(REC_ROW_CHUNKS):
        rows = slice(rc * tr, (rc + 1) * tr)
        xrows = slice(halo + rc * tr, halo + (rc + 1) * tr)
        has_next = rc + 1 < REC_ROW_CHUNKS

        xc = cb_ref[...] + cw[CONV_WIDTH - 1:CONV_WIDTH, :] * xbuf_ref[xrows, :]
        for back in range(1, CONV_WIDTH):
            w_row = cw[CONV_WIDTH - 1 - back:CONV_WIDTH - back, :]
            xc = xc + w_row * xbuf_ref[halo + rc * tr - back:halo + (rc + 1) * tr - back, :]
        xn_next = norm_rows(rc + 1) if has_next else None

        for n in range(N_RNN_BLOCKS):
            sl = slice(n * RNN_BLOCK, (n + 1) * RNN_BLOCK)
            xcn = xc[:, sl]
            xb = xcn.astype(BF16)
            r = jax.nn.sigmoid(
                jnp.dot(xb, wga_ref[n], preferred_element_type=F32) + bga_ref[:, sl])
            ig = jax.nn.sigmoid(
                jnp.dot(xb, wgx_ref[n], preferred_element_type=F32) + bgx_ref[:, sl])
            log_a = neg_c_softplus[:, sl] * r
            a = jnp.exp(log_a)
            one_minus_a2 = -jnp.tanh(log_a) * (a * a + 1.0)
            a_ref[rows, sl] = a
            b_ref[rows, sl] = jnp.sqrt(one_minus_a2) * (ig * xcn)
            if n % 2 == 1:
                gate_piece(rc, n // 2, xn)

        for cblk in range(D_RNN // SCAN_COLS):
            if has_next:
                recurrent_piece(rc + 1, cblk, xn_next)
            cs = slice(cblk * SCAN_COLS, (cblk + 1) * SCAN_COLS)
            h_prev = hcar_ref[:, cs]
            for t in range(rc * tr // SUBLANES, (rc + 1) * tr // SUBLANES):
                trows = slice(t * SUBLANES, (t + 1) * SUBLANES)
                a = a_ref[trows, cs]
                b = b_ref[trows, cs]
                for d in (1, 2, 4):
                    keep = sub >= d
                    a_sh = jnp.where(keep, pltpu.roll(a, d, axis=0), 1.0)
                    b_sh = jnp.where(keep, pltpu.roll(b, d, axis=0), 0.0)
                    b = a * b_sh + b
                    a = a * a_sh
                h = a * h_prev + b
                b_ref[trows, cs] = h
                h_prev = jnp.broadcast_to(h[SUBLANES - 1:SUBLANES, :],
                                          (SUBLANES, SCAN_COLS))
            hcar_ref[:, cs] = h_prev

        y = jax.nn.gelu(gate_ref[rows, :], approximate=True)
        o_ref[0, rows, :] = (b_ref[rows, :] * y).astype(o_ref.dtype)
        xn = xn_next

    xbuf_ref[0:halo, :] = xbuf_ref[ts:ts + halo, :]


def _rglru(h, g, w_in, conv_w, conv_b, w_ga, b_ga, w_gx, b_gx, lam):
    batch, seq, d = h.shape
    ts = TS_REC
    row = lambda: pl.BlockSpec((1, D_RNN), lambda b, i: (0, 0))
    wblk = lambda: _resident((N_RNN_BLOCKS, RNN_BLOCK, RNN_BLOCK), lambda b, i: (0, 0, 0))
    return pl.pallas_call(
        _rglru_kernel,
        grid=(batch, seq // ts),
        in_specs=[
            pl.BlockSpec((1, ts, d), lambda b, i: (b, i, 0)),
            pl.BlockSpec((1, d), lambda b, i: (0, 0)),
            _resident((d, 2 * D_RNN), lambda b, i: (0, 0)),
            pl.BlockSpec((CONV_WIDTH, D_RNN), lambda b, i: (0, 0)),
            row(), wblk(), row(), wblk(), row(), row(),
        ],
        out_specs=pl.BlockSpec((1, ts, D_RNN), lambda b, i: (b, i, 0)),
        out_shape=jax.ShapeDtypeStruct((batch, seq, D_RNN), BF16),
        scratch_shapes=[
            pltpu.VMEM((ts + SUBLANES, D_RNN), F32),
            pltpu.VMEM((ts, D_RNN), F32),
            pltpu.VMEM((ts, D_RNN), F32),
            pltpu.VMEM((ts, D_RNN), F32),
            pltpu.VMEM((SUBLANES, D_RNN), F32),
        ],
        compiler_params=_compiler_params(2),
        name="rglru",
    )(h, g, w_in, conv_w, conv_b, w_ga, b_ga, w_gx, b_gx, lam)


def _lambda_init(layer_idx):
    return 0.8 - 0.6 * math.exp(-0.3 * layer_idx)


def kernel(x, p, g_mix, g_mlp, g_ple, w_qkv, g_q, g_k, lam_q1, lam_k1, lam_q2, lam_k2,
           g_sub, w_o_attn, w_in_rec, conv_w, conv_b, w_gate_a, b_gate_a, w_gate_x,
           b_gate_x, lam_rec, w_o_rec, w_up, w_down, w_ple_proj, w_ple_gate):
    batch, seq, d = x.shape
    depth = p.shape[0]
    t = batch * seq
    h = x.reshape(t, d)
    row = lambda v: v.reshape(1, -1)
    slopes = jnp.exp2(-8.0 * jnp.arange(1, N_HEADS + 1, dtype=F32) / N_HEADS)
    w_qkv_b = w_qkv.astype(BF16)
    w_up_b = w_up.astype(BF16)
    w_down_b = w_down.astype(BF16)
    w_ple_gate_b = w_ple_gate.astype(BF16)
    w_ple_proj_b = w_ple_proj.astype(BF16)
    p_rows = p.reshape(depth, t, -1)

    for i in range(depth):
        j = i // 2
        if i % 2 == 0:
            q, k, vt = _qkv_proj(h, row(g_mix[i]), w_qkv_b, j, row(g_q[j]), row(g_k[j]))
            o = _diff_attention(
                q.reshape(batch, seq, -1), k.reshape(batch, seq, -1), vt, slopes,
                row(lam_q1[j]), row(lam_k1[j]), row(lam_q2[j]), row(lam_k2[j]),
                g_sub[j].reshape(-1, 1), _lambda_init(i), batch, seq)
            h = _matmul_residual(o.reshape(t, -1), w_o_attn[j].astype(BF16), h)
        else:
            hy = _rglru(
                h.reshape(batch, seq, d), row(g_mix[i]), w_in_rec[j].astype(BF16), conv_w[j],
                row(conv_b[j]), w_gate_a[j].astype(BF16), row(b_gate_a[j]),
                w_gate_x[j].astype(BF16), row(b_gate_x[j]), row(lam_rec[j]))
            h = _matmul_residual(hy.reshape(t, -1), w_o_rec[j].astype(BF16), h)
        h = _mlp(h, row(g_mlp[i]), w_up_b, w_down_b, i)
        h = _ple(h, row(g_ple[i]), w_ple_gate_b, p_rows, w_ple_proj_b, i)
    return h.reshape(batch, seq, d)
```
